```python
import math
import jax
import jax.numpy as jnp
from jax import lax
import numpy as np

D_MODEL = 4096
BATCH = 2
SEQ = 8192
DEPTH = 2

CHUNK = 64
EPS = 1e-6
CONV_CH = D_MODEL // 2
CONV_WIDTH = 31
SB_HEADS = 16
SB_HEAD_DIM = D_MODEL // 2 // SB_HEADS
SB_WIDTH = SB_HEADS * SB_HEAD_DIM
Q_BLOCK = 128
EVEN_IN_COLS = 2 * CONV_CH + 3 * SB_WIDTH
EVEN_MIX_WIDTH = CONV_CH + SB_WIDTH
S5_WIDTH = D_MODEL
S5_GROUP = 16
S5_GROUPS = S5_WIDTH // S5_GROUP
S5_STATE = 64
DT_MIN = 1e-3
DT_MAX = 1e-1
N_EXPERTS = 64
TOP_K = 8
N_GROUPS = 8
TOPK_GROUPS = 4
EXPERT_DIM = D_MODEL // 16
SHARED_DIM = D_MODEL // 4
ROUTED_SCALE = 2.5
MOE_BLOCK = 128

kernel_name = 'hybrid_conv_stickbreak_s5_moe_trunk'


def rms_norm(x, g):
    xf = x.astype(jnp.float32)
    y = xf * lax.rsqrt(jnp.mean(xf * xf, axis=-1, keepdims=True) + EPS)
    return y.astype(x.dtype) * g


def layer_norm(x, g, b):
    xf = x.astype(jnp.float32)
    mu = jnp.mean(xf, axis=-1, keepdims=True)
    var = jnp.mean(jnp.square(xf - mu), axis=-1, keepdims=True)
    return ((xf - mu) * lax.rsqrt(var + EPS)).astype(x.dtype) * g + b


def adaln(c, w, b):
    mod = jax.nn.silu(c) @ w + b
    shift, scale, gate = jnp.split(mod, 3, axis=-1)
    return shift[:, None, :], scale[:, None, :], gate[:, None, :]


def causal_depthwise_conv(u, w, b):
    out = lax.conv_general_dilated(
        u, w[:, None, :], window_strides=(1,), padding=[(CONV_WIDTH - 1, 0)],
        dimension_numbers=('NWC', 'WIO', 'NWC'), feature_group_count=u.shape[-1])
    return out + b


def stick_breaking_attention(q, k, v):
    bsz, s, h, dh = q.shape
    nb = s // Q_BLOCK
    qh = q.transpose(0, 2, 1, 3)
    kh = k.transpose(0, 2, 1, 3)
    vh = v.transpose(0, 2, 1, 3)
    q_blocks = qh.reshape(bsz, h, nb, Q_BLOCK, dh).transpose(2, 0, 1, 3, 4)
    key_pos = jnp.arange(s)
    inv_sqrt = 1.0 / math.sqrt(dh)

    def block(args):
        qb, start = args
        z = jnp.einsum('bhqd,bhkd->bhqk', qb, kh).astype(jnp.float32) * inv_sqrt
        q_pos = start + jnp.arange(Q_BLOCK)
        strict = key_pos[None, :] < q_pos[:, None]
        log_keep = jnp.where(strict, jax.nn.log_sigmoid(-z), 0.0)
        later = lax.cumsum(log_keep, axis=3, reverse=True) - log_keep
        w = jnp.where(strict, jnp.exp(jax.nn.log_sigmoid(z) + later), 0.0)
        return jnp.einsum('bhqk,bhkd->bhqd', w.astype(vh.dtype), vh)

    out = lax.map(block, (q_blocks, jnp.arange(nb) * Q_BLOCK))
    return out.transpose(1, 0, 3, 2, 4).reshape(bsz, s, h * dh)


def conv_stickbreak_mixer(h, w_in, conv_w, conv_b, ln_g, ln_b, q_norm_g, k_norm_g, w_out):
    bsz, s, _ = h.shape
    proj = h @ w_in
    a_val, a_gate, q, k, v = jnp.split(
        proj, [CONV_CH, 2 * CONV_CH, 2 * CONV_CH + SB_WIDTH, 2 * CONV_CH + 2 * SB_WIDTH], axis=-1)
    a = a_val * jax.nn.sigmoid(a_gate)
    a = jax.nn.silu(layer_norm(causal_depthwise_conv(a, conv_w, conv_b), ln_g, ln_b))
    heads = (bsz, s, SB_HEADS, SB_HEAD_DIM)
    q = rms_norm(q.reshape(heads), q_norm_g)
    k = rms_norm(k.reshape(heads), k_norm_g)
    o = stick_breaking_attention(q, k, v.reshape(heads))
    return jnp.concatenate([a, o], axis=-1) @ w_out


def s5_mixer(h, w_in, log_dt, lam_re, lam_im, b_re, b_im, c_re, c_im, d_skip, w_out):
    bsz, s, _ = h.shape
    u = (h @ w_in).reshape(bsz, s, S5_GROUPS, S5_GROUP)
    dt = jnp.exp(log_dt)[:, None]
    mag = jnp.exp(lam_re * dt)
    ang = lam_im * dt
    lbar_re = mag * jnp.cos(ang)
    lbar_im = mag * jnp.sin(ang)
    den = lam_re * lam_re + lam_im * lam_im
    nr = lbar_re - 1.0
    f_re = (nr * lam_re + lbar_im * lam_im) / den
    f_im = (lbar_im * lam_re - nr * lam_im) / den
    bbar_re = f_re[..., None] * b_re - f_im[..., None] * b_im
    bbar_im = f_re[..., None] * b_im + f_im[..., None] * b_re
    nc = s // CHUNK
    u_chunks = u.reshape(bsz, nc, CHUNK, S5_GROUPS, S5_GROUP).transpose(1, 0, 2, 3, 4)
    a_re = jnp.broadcast_to(lbar_re, (bsz, CHUNK, S5_GROUPS, S5_STATE))
    a_im = jnp.broadcast_to(lbar_im, (bsz, CHUNK, S5_GROUPS, S5_STATE))

    def combine(e1, e2):
        ar1, ai1, br1, bi1 = e1
        ar2, ai2, br2, bi2 = e2
        return (ar2 * ar1 - ai2 * ai1, ar2 * ai1 + ai2 * ar1,
                ar2 * br1 - ai2 * bi1 + br2, ar2 * bi1 + ai2 * br1 + bi2)

    def step(carry, uc):
        h_re, h_im = carry
        bu_re = jnp.einsum('gph,blgh->blgp', bbar_re, uc)
        bu_im = jnp.einsum('gph,blgh->blgp', bbar_im, uc)
        bu_re = bu_re.at[:, 0].add(lbar_re * h_re - lbar_im * h_im)
        bu_im = bu_im.at[:, 0].add(lbar_re * h_im + lbar_im * h_re)
        _, _, s_re, s_im = lax.associative_scan(combine, (a_re, a_im, bu_re, bu_im), axis=1)
        y = (jnp.einsum('ghp,blgp->blgh', c_re, s_re)
             - jnp.einsum('ghp,blgp->blgh', c_im, s_im) + d_skip * uc)
        return (s_re[:, -1].astype(h_re.dtype), s_im[:, -1].astype(h_im.dtype)), y

    h0 = jnp.zeros((bsz, S5_GROUPS, S5_STATE), u.dtype)
    _, ys = lax.scan(step, (h0, h0), u_chunks)
    y = jax.nn.gelu(ys.transpose(1, 0, 2, 3, 4).reshape(bsz, s, S5_WIDTH))
    ab = y @ w_out
    return ab[..., :D_MODEL] * jax.nn.sigmoid(ab[..., D_MODEL:])


def moe_ffn(h, router_w, router_bias, w_gate, w_up, w_down, s_gate, s_up, s_down):
    bsz, s, d = h.shape
    xf = h.reshape(-1, d)
    t = xf.shape[0]
    scores = jax.nn.sigmoid((xf @ router_w).astype(jnp.float32))
    biased = scores + router_bias.astype(jnp.float32)
    per_group = N_EXPERTS // N_GROUPS
    group_score = lax.top_k(biased.reshape(t, N_GROUPS, per_group), 2)[0].sum(-1)
    _, top_groups = lax.top_k(group_score, TOPK_GROUPS)
    group_ok = jnp.any(top_groups[:, :, None] == jnp.arange(N_GROUPS)[None, None, :], axis=1)
    expert_ok = jnp.repeat(group_ok, per_group, axis=1)
    _, idx = lax.top_k(jnp.where(expert_ok, biased, -jnp.inf), TOP_K)
    gates = jnp.take_along_axis(scores, idx, axis=1)
    gates = gates / jnp.sum(gates, axis=-1, keepdims=True) * ROUTED_SCALE
    tk = t * TOP_K
    flat_e = idx.reshape(-1)
    order = jnp.argsort(flat_e)
    sorted_e = flat_e[order]
    counts = jnp.bincount(flat_e, length=N_EXPERTS)
    padded = (counts + MOE_BLOCK - 1) // MOE_BLOCK * MOE_BLOCK
    pad_end = jnp.cumsum(padded)
    pad_start = pad_end - padded
    start = jnp.cumsum(counts) - counts
    dest = pad_start[sorted_e] + jnp.arange(tk, dtype=jnp.int32) - start[sorted_e]
    n_rows = tk + N_EXPERTS * MOE_BLOCK
    n_blocks = n_rows // MOE_BLOCK
    row_tok = jnp.full((n_rows,), t, jnp.int32).at[dest].set((order // TOP_K).astype(jnp.int32))
    row_gate = jnp.zeros((n_rows,), xf.dtype).at[dest].set(gates.reshape(-1)[order].astype(xf.dtype))
    block_expert = jnp.minimum(
        jnp.searchsorted(pad_end, jnp.arange(n_blocks) * MOE_BLOCK, side='right'), N_EXPERTS - 1)
    x_pad = jnp.concatenate([xf, jnp.zeros((1, d), xf.dtype)], axis=0)

    def expert_block(acc, blk):
        rows, g, e = blk
        xb = x_pad[rows]
        hb = jax.nn.silu(xb @ w_gate[e]) * (xb @ w_up[e])
        return acc.at[rows].add((hb @ w_down[e]) * g[:, None]), None

    routed, _ = lax.scan(
        expert_block, jnp.zeros((t + 1, d), xf.dtype),
        (row_tok.reshape(n_blocks, MOE_BLOCK), row_gate.reshape(n_blocks, MOE_BLOCK), block_expert))
    shared = (jax.nn.silu(xf @ s_gate) * (xf @ s_up)) @ s_down
    return (routed[:t] + shared).reshape(bsz, s, d)


def setup_inputs(seed: int = 0) -> dict:
    key = jax.random.key(seed)
    ks = iter(jax.random.split(key, 48))
    D = D_MODEL
    n_even = (DEPTH + 1) // 2
    n_odd = DEPTH // 2

    def nrm(shape, scale):
        return jax.random.normal(next(ks), shape, jnp.float32) * scale

    inp = {}
    inp['x'] = nrm((BATCH, SEQ, D), 1.0)
    inp['c'] = nrm((BATCH, D), 1.0)
    inp['mix_norm_g'] = 1.0 + nrm((DEPTH, D), 0.02)
    inp['mix_ada_w'] = nrm((DEPTH, D, 3 * D), 0.5 * D ** -0.5)
    inp['mix_ada_b'] = nrm((DEPTH, 3 * D), 0.02)
    inp['ev_w_in'] = nrm((n_even, D, EVEN_IN_COLS), D ** -0.5)
    inp['ev_conv_w'] = nrm((n_even, CONV_WIDTH, CONV_CH), CONV_WIDTH ** -0.5)
    inp['ev_conv_b'] = nrm((n_even, CONV_CH), 0.02)
    inp['ev_ln_g'] = 1.0 + nrm((n_even, CONV_CH), 0.02)
    inp['ev_ln_b'] = nrm((n_even, CONV_CH), 0.02)
    inp['ev_q_norm_g'] = 1.0 + nrm((n_even, SB_HEAD_DIM), 0.02)
    inp['ev_k_norm_g'] = 1.0 + nrm((n_even, SB_HEAD_DIM), 0.02)
    inp['ev_w_out'] = nrm((n_even, EVEN_MIX_WIDTH, D), EVEN_MIX_WIDTH ** -0.5)
    inp['od_w_in'] = nrm((n_odd, D, S5_WIDTH), D ** -0.5)
    inp['od_log_dt'] = jax.random.uniform(next(ks), (n_odd, S5_GROUPS), jnp.float32,
                                          minval=math.log(DT_MIN), maxval=math.log(DT_MAX))
    inp['od_lambda_re'] = -0.5 + nrm((n_odd, S5_GROUPS, S5_STATE), 0.01)
    inp['od_lambda_im'] = (jnp.pi * jnp.arange(S5_STATE, dtype=jnp.float32)
                           + nrm((n_odd, S5_GROUPS, S5_STATE), 0.01))
    inp['od_b_re'] = nrm((n_odd, S5_GROUPS, S5_STATE, S5_GROUP), (2 * S5_GROUP) ** -0.5)
    inp['od_b_im'] = nrm((n_odd, S5_GROUPS, S5_STATE, S5_GROUP), (2 * S5_GROUP) ** -0.5)
    inp['od_c_re'] = nrm((n_odd, S5_GROUPS, S5_GROUP, S5_STATE), (2 * S5_STATE) ** -0.5)
    inp['od_c_im'] = nrm((n_odd, S5_GROUPS, S5_GROUP, S5_STATE), (2 * S5_STATE) ** -0.5)
    inp['od_d'] = nrm((n_odd, S5_GROUPS, S5_GROUP), 0.5)
    inp['od_w_out'] = nrm((n_odd, S5_WIDTH, 2 * D), S5_WIDTH ** -0.5)
    inp['ffn_norm_g'] = 1.0 + nrm((DEPTH, D), 0.02)
    inp['ffn_ada_w'] = nrm((DEPTH, D, 3 * D), 0.5 * D ** -0.5)
    inp['ffn_ada_b'] = nrm((DEPTH, 3 * D), 0.02)
    inp['router_w'] = nrm((DEPTH, D, N_EXPERTS), D ** -0.5)
    inp['router_bias'] = nrm((DEPTH, N_EXPERTS), 0.01)
    inp['exp_gate'] = nrm((DEPTH, N_EXPERTS, D, EXPERT_DIM), D ** -0.5)
    inp['exp_up'] = nrm((DEPTH, N_EXPERTS, D, EXPERT_DIM), D ** -0.5)
    inp['exp_down'] = nrm((DEPTH, N_EXPERTS, EXPERT_DIM, D), EXPERT_DIM ** -0.5)
    inp['sh_gate'] = nrm((DEPTH, D, SHARED_DIM), D ** -0.5)
    inp['sh_up'] = nrm((DEPTH, D, SHARED_DIM), D ** -0.5)
    inp['sh_down'] = nrm((DEPTH, SHARED_DIM, D), SHARED_DIM ** -0.5)
    return inp


def reference(x, c, mix_norm_g, mix_ada_w, mix_ada_b, ev_w_in, ev_conv_w, ev_conv_b, ev_ln_g, ev_ln_b,
              ev_q_norm_g, ev_k_norm_g, ev_w_out, od_w_in, od_log_dt, od_lambda_re, od_lambda_im,
              od_b_re, od_b_im, od_c_re, od_c_im, od_d, od_w_out, ffn_norm_g, ffn_ada_w, ffn_ada_b,
              router_w, router_bias, exp_gate, exp_up, exp_down, sh_gate, sh_up, sh_down):
    for i in range(DEPTH):
        j = i // 2
        shift, scale, gate = adaln(c, mix_ada_w[i], mix_ada_b[i])
        hn = rms_norm(x, mix_norm_g[i]) * (1.0 + scale) + shift
        if i % 2 == 0:
            m = conv_stickbreak_mixer(hn, ev_w_in[j], ev_conv_w[j], ev_conv_b[j], ev_ln_g[j], ev_ln_b[j],
                                      ev_q_norm_g[j], ev_k_norm_g[j], ev_w_out[j])
        else:
            m = s5_mixer(hn, od_w_in[j], od_log_dt[j], od_lambda_re[j], od_lambda_im[j], od_b_re[j],
                         od_b_im[j], od_c_re[j], od_c_im[j], od_d[j], od_w_out[j])
        x = x + gate * m
        shift, scale, gate = adaln(c, ffn_ada_w[i], ffn_ada_b[i])
        hn = rms_norm(x, ffn_norm_g[i]) * (1.0 + scale) + shift
        x = x + gate * moe_ffn(hn, router_w[i], router_bias[i], exp_gate[i], exp_up[i], exp_down[i],
                               sh_gate[i], sh_up[i], sh_down[i])
    return x
```

```python
import functools
import math

import jax
import jax.numpy as jnp
from jax import lax
from jax.experimental import pallas as pl
from jax.experimental.pallas import tpu as pltpu

EPS = 1e-6
TOP_K = 8
N_GROUPS = 8
TOPK_GROUPS = 4
ROUTED_SCALE = 2.5
S5_CHUNK = 8
S5_BLOCK_LANES = 128

V7X_VMEM_LIMIT = 56 * 1024 * 1024


def _cp(dims, vmem=V7X_VMEM_LIMIT):
    return pltpu.CompilerParams(dimension_semantics=dims, vmem_limit_bytes=vmem)


def _sigmoid(x):
    return 1.0 / (1.0 + jnp.exp(-x))


def _bf16(x):
    return x.astype(jnp.bfloat16)


def _dot(a, b):
    return jnp.dot(a, b, preferred_element_type=jnp.float32)


def _tile(n, target, quantum=128):
    if n <= target:
        return n
    best = quantum
    for cand in range(quantum, target + 1, quantum):
        if n % cand == 0:
            best = cand
    assert n % best == 0, (n, target)
    return best


def _dot_nt(a, b):
    return lax.dot_general(a, b, (((1,), (1,)), ((), ())), preferred_element_type=jnp.float32)


def _adaln_kernel(c_ref, w_ref, b_ref, o_ref):
    c = c_ref[...]
    sc = _bf16(c * _sigmoid(c))
    o_ref[...] = _dot(sc, _bf16(w_ref[...])) + b_ref[...]


def adaln_all(c_pad, w, b, tn=512):
    nl, d, n = w.shape
    tn = min(tn, n)
    return pl.pallas_call(
        _adaln_kernel,
        grid=(nl, n // tn),
        in_specs=[
            pl.BlockSpec((8, d), lambda l, j: (0, 0)),
            pl.BlockSpec((None, d, tn), lambda l, j: (l, 0, j)),
            pl.BlockSpec((None, 1, tn), lambda l, j: (l, 0, j)),
        ],
        out_specs=pl.BlockSpec((None, 8, tn), lambda l, j: (l, 0, j)),
        out_shape=jax.ShapeDtypeStruct((nl, 8, n), jnp.float32),
        compiler_params=_cp(("parallel", "parallel")),
        name="adaln",
    )(c_pad, w, b.reshape(nl, 1, n))


def _pack_bf16_pair(a, b):
    ua = pltpu.bitcast(_bf16(a).astype(jnp.float32), jnp.uint32)
    ub = pltpu.bitcast(_bf16(b).astype(jnp.float32), jnp.uint32)
    return (ua & jnp.uint32(0xFFFF0000)) | (ub >> 16)


def _unpack_hi(p):
    return pltpu.bitcast(p & jnp.uint32(0xFFFF0000), jnp.float32)


def _unpack_lo(p):
    return pltpu.bitcast(p << 16, jnp.float32)


def _norm_mod(x, g_ref, sc_ref, sh_ref):
    xf = x.astype(jnp.float32)
    ms = jnp.mean(xf * xf, axis=-1, keepdims=True)
    return xf * lax.rsqrt(ms + EPS) * g_ref[...] * (1.0 + sc_ref[...]) + sh_ref[...]


def _norm_kernel(x_ref, g_ref, sc_ref, sh_ref, o_ref):
    o_ref[...] = _bf16(_norm_mod(x_ref[...], g_ref, sc_ref, sh_ref))


def _norm_pack_kernel(x_ref, g_ref, sc_ref, sh_ref, o_ref, p_ref):
    y = _norm_mod(x_ref[...], g_ref, sc_ref, sh_ref)
    o_ref[...] = _bf16(y)
    h = y.shape[-1] // 2
    p_ref[...] = _pack_bf16_pair(y[:, :h], y[:, h:])


def norm_modulate(x, g, scale, shift, packed=False, ts=256):
    b, s, d = x.shape
    ts = min(ts, s)
    nt = s // ts
    in_specs = [
        pl.BlockSpec((None, ts, d), lambda i, j: (i, j, 0)),
        pl.BlockSpec((1, d), lambda i, j: (0, 0)),
        pl.BlockSpec((None, 1, d), lambda i, j: (i, 0, 0)),
        pl.BlockSpec((None, 1, d), lambda i, j: (i, 0, 0)),
    ]
    o_spec = pl.BlockSpec((ts, d), lambda i, j: (i * nt + j, 0))
    o_shape = jax.ShapeDtypeStruct((b * s, d), jnp.bfloat16)
    if not packed:
        return pl.pallas_call(
            _norm_kernel, grid=(b, nt), in_specs=in_specs, out_specs=o_spec, out_shape=o_shape,
            compiler_params=_cp(("parallel", "parallel")), name="norm_mod",
        )(x, g.reshape(1, d), scale, shift)
    return pl.pallas_call(
        _norm_pack_kernel, grid=(b, nt), in_specs=in_specs,
        out_specs=[o_spec, pl.BlockSpec((ts, d // 2), lambda i, j: (i * nt + j, 0))],
        out_shape=[o_shape, jax.ShapeDtypeStruct((b * s, d // 2), jnp.uint32)],
        compiler_params=_cp(("parallel", "parallel")), name="norm_mod_pack",
    )(x, g.reshape(1, d), scale, shift)


def _mm_kernel(a_ref, w_ref, o_ref):
    o_ref[...] = _dot(_bf16(a_ref[...]), w_ref[...]).astype(o_ref.dtype)


def matmul(a, w, out_dtype, bm=1024, bn=1024):
    m, k = a.shape
    n = w.shape[1]
    bm, bn = _tile(m, bm), _tile(n, bn)
    return pl.pallas_call(
        _mm_kernel, grid=(m // bm, n // bn),
        in_specs=[pl.BlockSpec((bm, k), lambda i, j: (i, 0)), pl.BlockSpec((k, bn), lambda i, j: (0, j))],
        out_specs=pl.BlockSpec((bm, bn), lambda i, j: (i, j)),
        out_shape=jax.ShapeDtypeStruct((m, n), out_dtype),
        compiler_params=_cp(("parallel", "parallel")), name="matmul",
    )(a, w)


def _mm2_resid_kernel(a1_ref, a2_ref, w1_ref, w2_ref, x_ref, g_ref, o_ref):
    m = _dot(a1_ref[...], w1_ref[...]) + _dot(a2_ref[...], w2_ref[...])
    o_ref[...] = x_ref[...] + g_ref[...] * m


def matmul2_resid(a1, a2, w, x2d, gate, seq, bm=1024, bn=1024):
    m, k1 = a1.shape
    k2 = a2.shape[1]
    assert k1 == k2
    n = w.shape[1]
    bm, bn = _tile(seq, bm), _tile(n, bn)
    return pl.pallas_call(
        _mm2_resid_kernel, grid=(m // bm, n // bn),
        in_specs=[
            pl.BlockSpec((bm, k1), lambda i, j: (i, 0)),
            pl.BlockSpec((bm, k2), lambda i, j: (i, 0)),
            pl.BlockSpec((k1, bn), lambda i, j: (0, j)),
            pl.BlockSpec((k2, bn), lambda i, j: (1, j)),
            pl.BlockSpec((bm, bn), lambda i, j: (i, j)),
            pl.BlockSpec((None, 1, bn), lambda i, j: ((i * bm) // seq, 0, j)),
        ],
        out_specs=pl.BlockSpec((bm, bn), lambda i, j: (i, j)),
        out_shape=jax.ShapeDtypeStruct((m, n), jnp.float32),
        compiler_params=_cp(("parallel", "parallel")), name="matmul2_resid",
    )(a1, a2, w, w, x2d, gate)


def _mm_glu_resid_kernel(a_ref, wa_ref, wb_ref, x_ref, g_ref, o_ref):
    a = _bf16(a_ref[...])
    va = _dot(a, wa_ref[...])
    vb = _dot(a, wb_ref[...])
    o_ref[...] = x_ref[...] + g_ref[...] * (va * _sigmoid(vb))


def matmul_glu_resid(a, w, x2d, gate, seq, bm=512, bn=512):
    m, k = a.shape
    n = w.shape[1] // 2
    bm, bn = _tile(seq, bm), _tile(n, bn)
    nb = n // bn
    return pl.pallas_call(
        _mm_glu_resid_kernel, grid=(m // bm, nb),
        in_specs=[
            pl.BlockSpec((bm, k), lambda i, j: (i, 0)),
            pl.BlockSpec((k, bn), lambda i, j: (0, j)),
            pl.BlockSpec((k, bn), lambda i, j: (0, j + nb)),
            pl.BlockSpec((bm, bn), lambda i, j: (i, j)),
            pl.BlockSpec((None, 1, bn), lambda i, j: ((i * bm) // seq, 0, j)),
        ],
        out_specs=pl.BlockSpec((bm, bn), lambda i, j: (i, j)),
        out_shape=jax.ShapeDtypeStruct((m, n), jnp.float32),
        compiler_params=_cp(("parallel", "parallel")), name="matmul_glu_resid",
    )(a, w, w, x2d, gate)


def _mm_swiglu_kernel(a_ref, wg_ref, wu_ref, o_ref):
    a = a_ref[...]
    g = _dot(a, wg_ref[...])
    u = _dot(a, wu_ref[...])
    o_ref[...] = _bf16(g * _sigmoid(g) * u)


def matmul_swiglu(a, wg, wu, bm=1024, bn=512):
    m, k = a.shape
    n = wg.shape[1]
    bm, bn = _tile(m, bm), _tile(n, bn)
    return pl.pallas_call(
        _mm_swiglu_kernel, grid=(m // bm, n // bn),
        in_specs=[
            pl.BlockSpec((bm, k), lambda i, j: (i, 0)),
            pl.BlockSpec((k, bn), lambda i, j: (0, j)),
            pl.BlockSpec((k, bn), lambda i, j: (0, j)),
        ],
        out_specs=pl.BlockSpec((bm, bn), lambda i, j: (i, j)),
        out_shape=jax.ShapeDtypeStruct((m, n), jnp.bfloat16),
        compiler_params=_cp(("parallel", "parallel")), name="matmul_swiglu",
    )(a, wg, wu)


CONV_HALO = 32
CONV_ROWS = 64
CONV_LANES = 256


def _conv_kernel(v_ref, g_ref, vh_ref, gh_ref, w_ref, cb_ref, lg_ref, lb_ref, o_ref, abuf, cbuf, *,
                 tiles_per_seq, width):
    ts, cc = o_ref.shape
    first = (pl.program_id(0) % tiles_per_seq) == 0
    vh = vh_ref[...].astype(jnp.float32)
    gh = gh_ref[...].astype(jnp.float32)
    abuf[0:CONV_HALO, :] = jnp.where(first, 0.0, vh * _sigmoid(gh))
    v = v_ref[...].astype(jnp.float32)
    g = g_ref[...].astype(jnp.float32)
    abuf[CONV_HALO:, :] = v * _sigmoid(g)
    off = CONV_HALO - (width - 1)
    for lc in range(cc // CONV_LANES):
        ls = slice(lc * CONV_LANES, (lc + 1) * CONV_LANES)
        for rc in range(ts // CONV_ROWS):
            r0 = rc * CONV_ROWS
            acc = jnp.zeros((CONV_ROWS, CONV_LANES), jnp.float32)
            for j in range(width):
                acc = acc + w_ref[j:j + 1, ls] * abuf[r0 + off + j:r0 + off + j + CONV_ROWS, ls]
            cbuf[r0:r0 + CONV_ROWS, ls] = acc + cb_ref[:, ls]
    c = cbuf[...]
    mu = jnp.mean(c, axis=-1, keepdims=True)
    cen = c - mu
    var = jnp.mean(cen * cen, axis=-1, keepdims=True)
    y = cen * lax.rsqrt(var + EPS) * lg_ref[...] + lb_ref[...]
    o_ref[...] = _bf16(y * _sigmoid(y))


def conv_branch(proj, conv_w, conv_b, ln_g, ln_b, seq, ts=128):
    t = proj.shape[0]
    width, cc = conv_w.shape
    ts = min(ts, seq)
    assert width - 1 <= CONV_HALO and ts % CONV_HALO == 0 and cc % CONV_LANES == 0 and ts % CONV_ROWS == 0
    r = ts // CONV_HALO
    halo_idx = lambda i: jnp.maximum(i * r - 1, 0)
    kern = functools.partial(_conv_kernel, tiles_per_seq=seq // ts, width=width)
    return pl.pallas_call(
        kern, grid=(t // ts,),
        in_specs=[
            pl.BlockSpec((ts, cc), lambda i: (i, 0)),
            pl.BlockSpec((ts, cc), lambda i: (i, 1)),
            pl.BlockSpec((CONV_HALO, cc), lambda i: (halo_idx(i), 0)),
            pl.BlockSpec((CONV_HALO, cc), lambda i: (halo_idx(i), 1)),
            pl.BlockSpec((width, cc), lambda i: (0, 0)),
            pl.BlockSpec((1, cc), lambda i: (0, 0)),
            pl.BlockSpec((1, cc), lambda i: (0, 0)),
            pl.BlockSpec((1, cc), lambda i: (0, 0)),
        ],
        out_specs=pl.BlockSpec((ts, cc), lambda i: (i, 0)),
        out_shape=jax.ShapeDtypeStruct((t, cc), jnp.bfloat16),
        scratch_shapes=[pltpu.VMEM((CONV_HALO + ts, cc), jnp.float32), pltpu.VMEM((ts, cc), jnp.float32)],
        compiler_params=_cp(("parallel",)), name="conv_branch",
    )(proj, proj, proj, proj, conv_w, conv_b.reshape(1, cc), ln_g.reshape(1, cc), ln_b.reshape(1, cc))


def _qknorm_kernel(q_ref, k_ref, gq_ref, gk_ref, qo_ref, ko_ref, *, dh):
    inv_sqrt = 1.0 / math.sqrt(dh)
    for src, g_ref, dst, mul in ((q_ref, gq_ref, qo_ref, inv_sqrt), (k_ref, gk_ref, ko_ref, 1.0)):
        for h in range(src.shape[1] // dh):
            ls = slice(h * dh, (h + 1) * dh)
            xf = src[:, ls].astype(jnp.float32)
            ms = jnp.mean(xf * xf, axis=-1, keepdims=True)
            dst[:, ls] = _bf16(xf * lax.rsqrt(ms + EPS) * g_ref[...] * mul)


def qk_norm(proj, gq, gk, col0, width, ts=512):
    t = proj.shape[0]
    dh = gq.shape[0]
    ts = min(ts, t)
    qb, kb = col0 // width, col0 // width + 1
    return pl.pallas_call(
        functools.partial(_qknorm_kernel, dh=dh), grid=(t // ts,),
        in_specs=[
            pl.BlockSpec((ts, width), lambda i: (i, qb)),
            pl.BlockSpec((ts, width), lambda i: (i, kb)),
            pl.BlockSpec((1, dh), lambda i: (0, 0)),
            pl.BlockSpec((1, dh), lambda i: (0, 0)),
        ],
        out_specs=[pl.BlockSpec((ts, width), lambda i: (i, 0)), pl.BlockSpec((ts, width), lambda i: (i, 0))],
        out_shape=[jax.ShapeDtypeStruct((t, width), jnp.bfloat16)] * 2,
        compiler_params=_cp(("parallel",)), name="qk_norm",
    )(proj, proj, gq.reshape(1, dh), gk.reshape(1, dh))


ATT_TILE = 512
ATT_SUB = 256


def _attn_kernel(qi_ref, kj_ref, q_ref, k_ref, v_ref, u_ref, o_ref, acc_ref, r_ref):
    p = pl.program_id(2)
    qi = qi_ref[p]
    kj = kj_ref[p]
    tq = q_ref.shape[0]
    tk = k_ref.shape[0]
    sub = u_ref.shape[1]

    @pl.when(kj == qi)
    def _():
        acc_ref[...] = jnp.zeros_like(acc_ref)
        r_ref[...] = jnp.zeros_like(r_ref)

    def process(diagonal):
        q = q_ref[...]
        for sb in reversed(range(tk // sub)):
            k = k_ref[sb * sub:(sb + 1) * sub, :]
            v = v_ref[sb * sub:(sb + 1) * sub, :]
            z = _dot_nt(q, k)
            lk = -(jnp.maximum(z, 0.0) + jnp.log(1.0 + jnp.exp(-jnp.abs(z))))
            if diagonal:
                row = lax.broadcasted_iota(jnp.int32, (tq, sub), 0)
                col = lax.broadcasted_iota(jnp.int32, (tq, sub), 1) + sb * sub
                valid = col < row
                lm = jnp.where(valid, lk, 0.0)
            else:
                lm = lk
            hi = _bf16(lm)
            lo = _bf16(lm - hi.astype(jnp.float32))
            later = _dot(jnp.concatenate([hi, lo], axis=1), u_ref[...]) + r_ref[...]
            a = jnp.exp(z + lk + later)
            if diagonal:
                a = jnp.where(valid, a, 0.0)
            acc_ref[...] += _dot(_bf16(a), v)
            r_ref[...] += jnp.sum(lm, axis=1, keepdims=True)

    @pl.when(kj == qi)
    def _():
        process(True)

    @pl.when(kj != qi)
    def _():
        process(False)

    @pl.when(kj == 0)
    def _():
        o_ref[...] = acc_ref[...].astype(o_ref.dtype)


def stickbreak_attention(qn, kn, proj, vcol0, batch, seq, dh):
    t, w = qn.shape
    heads = w // dh
    tile = min(ATT_TILE, seq)
    sub = min(ATT_SUB, tile)
    nq = seq // tile
    pairs = [(i, j) for i in range(nq) for j in range(i, -1, -1)]
    qi = jnp.asarray([p[0] for p in pairs], jnp.int32)
    kj = jnp.asarray([p[1] for p in pairs], jnp.int32)
    r = lax.broadcasted_iota(jnp.int32, (sub, sub), 0)
    c = lax.broadcasted_iota(jnp.int32, (sub, sub), 1)
    tri = _bf16((r > c).astype(jnp.float32))
    u2 = jnp.concatenate([tri, tri], axis=0)
    vb = vcol0 // dh
    grid_spec = pltpu.PrefetchScalarGridSpec(
        num_scalar_prefetch=2,
        grid=(batch, heads, len(pairs)),
        in_specs=[
            pl.BlockSpec((tile, dh), lambda b, h, p, qi, kj: (b * nq + qi[p], h)),
            pl.BlockSpec((tile, dh), lambda b, h, p, qi, kj: (b * nq + kj[p], h)),
            pl.BlockSpec((tile, dh), lambda b, h, p, qi, kj: (b * nq + kj[p], vb + h)),
            pl.BlockSpec((2 * sub, sub), lambda b, h, p, qi, kj: (0, 0)),
        ],
        out_specs=pl.BlockSpec((tile, dh), lambda b, h, p, qi, kj: (b * nq + qi[p], h)),
        scratch_shapes=[pltpu.VMEM((tile, dh), jnp.float32), pltpu.VMEM((tile, 1), jnp.float32)],
    )
    return pl.pallas_call(
        _attn_kernel, grid_spec=grid_spec,
        out_shape=jax.ShapeDtypeStruct((t, w), jnp.bfloat16),
        compiler_params=_cp(("parallel", "parallel", "arbitrary")), name="stickbreak_attn",
    )(qi, kj, qn, kn, proj, u2)


def _gelu_tanh(x):
    return 0.5 * x * (1.0 + jnp.tanh(math.sqrt(2.0 / math.pi) * (x + 0.044715 * (x * x * x))))


def _s5_kernel(u_ref, wm_ref, we_ref, wc_ref, lam_ref, y_ref, s_ref, e_ref, sp_ref):
    tt, bl = u_ref.shape
    nch = tt // S5_CHUNK
    half = s_ref.shape[1]

    @pl.when(pl.program_id(2) == 0)
    def _():
        s_ref[...] = jnp.zeros_like(s_ref)

    ucat = _bf16(jnp.concatenate(
        [u_ref[pl.ds(j, nch, stride=S5_CHUNK), :] for j in range(S5_CHUNK)], axis=1))
    e_ref[...] = _dot(ucat, we_ref[...])
    lam_re = lam_ref[0:1, :]
    lam_im = lam_ref[1:2, :]

    def body(c, carry):
        sre, sim = carry
        sp_ref[pl.ds(c, 1), 0:half] = sre
        sp_ref[pl.ds(c, 1), half:] = sim
        ere = e_ref[pl.ds(c, 1), 0:half]
        eim = e_ref[pl.ds(c, 1), half:]
        return (lam_re * sre - lam_im * sim + ere, lam_re * sim + lam_im * sre + eim)

    sre, sim = lax.fori_loop(0, nch, body, (s_ref[0:1, :], s_ref[1:2, :]))
    s_ref[0:1, :] = sre
    s_ref[1:2, :] = sim
    ycat = _dot(ucat, wm_ref[...]) + _dot(_bf16(sp_ref[...]), wc_ref[...])
    for j in range(S5_CHUNK):
        y_ref[pl.ds(j, nch, stride=S5_CHUNK), :] = _gelu_tanh(ycat[:, j * bl:(j + 1) * bl])


def s5_weights(log_dt, lam_re, lam_im, b_re, b_im, c_re, c_im, d_skip):
    g, p = lam_re.shape
    hg = b_re.shape[-1]
    L = S5_CHUNK
    gb = S5_BLOCK_LANES // hg
    nb = g // gb
    dt = jnp.exp(log_dt)[:, None]
    mag = jnp.exp(lam_re * dt)
    ang = lam_im * dt
    lbar_re = mag * jnp.cos(ang)
    lbar_im = mag * jnp.sin(ang)
    den = lam_re * lam_re + lam_im * lam_im
    nr = lbar_re - 1.0
    f_re = (nr * lam_re + lbar_im * lam_im) / den
    f_im = (lbar_im * lam_re - nr * lam_im) / den
    bbar_re = f_re[..., None] * b_re - f_im[..., None] * b_im
    bbar_im = f_re[..., None] * b_im + f_im[..., None] * b_re
    n = jnp.arange(L + 1, dtype=jnp.float32)[:, None, None]
    pw_mag = jnp.exp(n * (lam_re * dt)[None])
    pw_re = pw_mag * jnp.cos(n * ang[None])
    pw_im = pw_mag * jnp.sin(n * ang[None])
    lb_re = pw_re[..., None] * bbar_re[None] - pw_im[..., None] * bbar_im[None]
    lb_im = pw_re[..., None] * bbar_im[None] + pw_im[..., None] * bbar_re[None]
    ktau = (jnp.einsum('gop,ngpi->ngoi', c_re, lb_re[:L], precision='highest')
            - jnp.einsum('gop,ngpi->ngoi', c_im, lb_im[:L], precision='highest'))
    jj = jnp.arange(L)[:, None]
    tt = jnp.arange(L)[None, :]
    tau = jnp.clip(tt - jj, 0, L - 1)
    kjt = jnp.where((tt >= jj)[..., None, None, None], ktau[tau], 0.0)
    kjt = kjt + (jj == tt)[..., None, None, None] * (d_skip[:, :, None] * jnp.eye(hg))[None, None]
    eye = jnp.eye(gb, dtype=jnp.float32)
    kb = kjt.reshape(L, L, nb, gb, hg, hg)
    w_main = jnp.einsum('jtbgoi,gh->bjgitho', kb, eye).reshape(nb, L * gb * hg, L * gb * hg)
    le_re = lb_re[L - 1 - jnp.arange(L)].reshape(L, nb, gb, p, hg)
    le_im = lb_im[L - 1 - jnp.arange(L)].reshape(L, nb, gb, p, hg)
    le = jnp.stack([le_re, le_im], axis=0)
    w_end = jnp.einsum('qjbgpi,gh->bjgiqhp', le, eye).reshape(nb, L * gb * hg, 2 * gb * p)
    m_re = c_re[None] * pw_re[1:, :, None, :] - c_im[None] * pw_im[1:, :, None, :]
    m_im = c_re[None] * pw_im[1:, :, None, :] + c_im[None] * pw_re[1:, :, None, :]
    mc = jnp.stack([m_re, -m_im], axis=0).reshape(2, L, nb, gb, hg, p)
    w_carry = jnp.einsum('qtbgop,gh->bqgptho', mc, eye).reshape(nb, 2 * gb * p, L * gb * hg)
    lam_l = jnp.stack([pw_re[L].reshape(nb, gb * p), pw_im[L].reshape(nb, gb * p)], axis=1)
    return _bf16(w_main), _bf16(w_end), _bf16(w_carry), lam_l


def s5_scan(u, w_main, w_end, w_carry, lam_l, tt=2048):
    b, s, w = u.shape
    nb, kdim, sdim = w_end.shape
    bl = S5_BLOCK_LANES
    tt = min(tt, s)
    nch = tt // S5_CHUNK
    return pl.pallas_call(
        _s5_kernel, grid=(b, nb, s // tt),
        in_specs=[
            pl.BlockSpec((None, tt, bl), lambda i, j, t: (i, t, j)),
            pl.BlockSpec((None, kdim, kdim), lambda i, j, t: (j, 0, 0)),
            pl.BlockSpec((None, kdim, sdim), lambda i, j, t: (j, 0, 0)),
            pl.BlockSpec((None, sdim, kdim), lambda i, j, t: (j, 0, 0)),
            pl.BlockSpec((None, 2, sdim // 2), lambda i, j, t: (j, 0, 0)),
        ],
        out_specs=pl.BlockSpec((None, tt, bl), lambda i, j, t: (i, t, j)),
        out_shape=jax.ShapeDtypeStruct((b, s, w), jnp.float32),
        scratch_shapes=[pltpu.VMEM((2, sdim // 2), jnp.float32), pltpu.VMEM((nch, sdim), jnp.float32),
                        pltpu.VMEM((nch, sdim), jnp.float32)],
        compiler_params=_cp(("parallel", "parallel", "arbitrary")), name="s5_scan",
    )(u, w_main, w_end, w_carry, lam_l)


def _router_kernel(h_ref, rw_ref, bias_ref, ut_ref, gate_ref, sel_ref, pos_ref, cnt_ref, run_ref):
    ne = rw_ref.shape[0]
    tt = h_ref.shape[0]
    per = ne // N_GROUPS

    @pl.when(pl.program_id(0) == 0)
    def _():
        run_ref[...] = jnp.zeros_like(run_ref)

    s = _sigmoid(_dot_nt(rw_ref[...], h_ref[...]))
    bsc = s + bias_ref[...]
    b3 = bsc.reshape(N_GROUPS, per, tt)
    m1 = jnp.max(b3, axis=1, keepdims=True)
    is1 = b3 == m1
    n1 = jnp.sum(is1.astype(jnp.float32), axis=1, keepdims=True)
    m2 = jnp.max(jnp.where(is1, -jnp.inf, b3), axis=1, keepdims=True)
    gs = (m1 + jnp.where(n1 >= 2.0, m1, m2)).reshape(N_GROUPS, tt)
    gidx = lax.broadcasted_iota(jnp.int32, (N_GROUPS, tt), 0)
    grank = jnp.zeros((N_GROUPS, tt), jnp.float32)
    for g in range(N_GROUPS):
        row = gs[g:g + 1, :]
        ahead = (row > gs) | ((row == gs) & (g < gidx))
        grank = grank + ahead.astype(jnp.float32)
    gok = (grank < float(TOPK_GROUPS)).astype(jnp.float32)
    eok = jnp.broadcast_to(gok.reshape(N_GROUPS, 1, tt), (N_GROUPS, per, tt)).reshape(ne, tt)
    masked = jnp.where(eok > 0.0, bsc, -jnp.inf)
    eidx = lax.broadcasted_iota(jnp.int32, (ne, tt), 0)
    erank = jnp.zeros((ne, tt), jnp.float32)
    for e in range(ne):
        row = masked[e:e + 1, :]
        ahead = (row > masked) | ((row == masked) & (e < eidx))
        erank = erank + ahead.astype(jnp.float32)
    sel = erank < float(TOP_K)
    self32 = sel.astype(jnp.float32)
    gsel = jnp.where(sel, s, 0.0)
    gsum = jnp.sum(gsel, axis=0, keepdims=True)
    gate_ref[...] = gsel / gsum * ROUTED_SCALE
    sel_ref[...] = self32
    pos_ref[...] = _dot(_bf16(self32), ut_ref[...]) + run_ref[...]
    run_ref[...] += jnp.sum(self32, axis=1, keepdims=True)
    cnt_ref[...] = jnp.broadcast_to(run_ref[...], cnt_ref.shape)


def moe_router(hn, router_w, router_bias, tt=512):
    t, d = hn.shape
    ne = router_w.shape[1]
    tt = min(tt, t)
    r = lax.broadcasted_iota(jnp.int32, (tt, tt), 0)
    c = lax.broadcasted_iota(jnp.int32, (tt, tt), 1)
    ut = _bf16((r < c).astype(jnp.float32))
    tok_spec = pl.BlockSpec((ne, tt), lambda i: (0, i))
    return pl.pallas_call(
        _router_kernel, grid=(t // tt,),
        in_specs=[
            pl.BlockSpec((tt, d), lambda i: (i, 0)),
            pl.BlockSpec((ne, d), lambda i: (0, 0)),
            pl.BlockSpec((ne, 1), lambda i: (0, 0)),
            pl.BlockSpec((tt, tt), lambda i: (0, 0)),
        ],
        out_specs=[tok_spec, tok_spec, tok_spec, pl.BlockSpec((ne, 128), lambda i: (0, 0))],
        out_shape=[jax.ShapeDtypeStruct((ne, t), jnp.float32)] * 3 + [jax.ShapeDtypeStruct((ne, 128), jnp.float32)],
        scratch_shapes=[pltpu.VMEM((ne, 1), jnp.float32)],
        compiler_params=_cp(("arbitrary",)), name="moe_router",
    )(hn, _bf16(router_w.T), router_bias.reshape(ne, 1).astype(jnp.float32), ut)


def _dispatch_kernel(last_ref, dest_ref, src_ref, out_ref, zbuf, sem, zsem, *, tt):
    base = pl.program_id(0) * tt
    bm = zbuf.shape[0]

    @pl.when(pl.program_id(0) == 0)
    def _():
        zbuf[...] = jnp.zeros_like(zbuf)

        def zero_copy(e):
            return pltpu.make_async_copy(zbuf, out_ref.at[pl.ds(pl.multiple_of(last_ref[e], bm), bm)], zsem)

        def zstart(e, carry):
            @pl.when(last_ref[e] >= 0)
            def _():
                zero_copy(e).start()
            return carry

        def zwait(e, carry):
            @pl.when(last_ref[e] >= 0)
            def _():
                zero_copy(e).wait()
            return carry

        lax.fori_loop(0, last_ref.shape[0], zstart, 0)
        lax.fori_loop(0, last_ref.shape[0], zwait, 0)

    def body(t, carry):
        for k in range(TOP_K):
            pltpu.make_async_copy(src_ref.at[pl.ds(base + t, 1)], out_ref.at[pl.ds(dest_ref[k, t], 1)], sem).start()
        return carry

    lax.fori_loop(0, tt, body, 0)
    n = TOP_K * tt
    pltpu.make_async_copy(src_ref.at[pl.ds(0, n)], out_ref.at[pl.ds(0, n)], sem).wait()


def moe_dispatch(zero_blocks, dest, hp, n_rows, bm, tt=256):
    t, w = hp.shape
    tt = min(tt, t)
    grid_spec = pltpu.PrefetchScalarGridSpec(
        num_scalar_prefetch=1, grid=(t // tt,),
        in_specs=[
            pl.BlockSpec((TOP_K, tt), lambda i, last: (0, i), memory_space=pltpu.SMEM),
            pl.BlockSpec(memory_space=pl.ANY),
        ],
        out_specs=pl.BlockSpec(memory_space=pl.ANY),
        scratch_shapes=[pltpu.VMEM((bm, w), jnp.uint32), pltpu.SemaphoreType.DMA(()), pltpu.SemaphoreType.DMA(())],
    )
    return pl.pallas_call(
        functools.partial(_dispatch_kernel, tt=tt), grid_spec=grid_spec,
        out_shape=jax.ShapeDtypeStruct((n_rows, w), jnp.uint32),
        compiler_params=pltpu.CompilerParams(dimension_semantics=("arbitrary",), has_side_effects=True),
        name="moe_dispatch",
    )(zero_blocks, dest, hp)


def _expert_kernel(be_ref, nu_ref, x_ref, wg_ref, wu_ref, wd_ref, o_ref, wg_s, wu_s, wd_s):
    b = pl.program_id(0)
    e = be_ref[b]
    prev = be_ref[jnp.maximum(b - 1, 0)]
    half = x_ref.shape[1]

    @pl.when((b == 0) | (e != prev))
    def _():
        wg_s[0] = _bf16(wg_ref[0:half, :])
        wg_s[1] = _bf16(wg_ref[half:, :])
        wu_s[0] = _bf16(wu_ref[0:half, :])
        wu_s[1] = _bf16(wu_ref[half:, :])
        wd_s[...] = _bf16(wd_ref[...])

    @pl.when(b < nu_ref[0])
    def _():
        xp = x_ref[...]
        xa = _bf16(_unpack_hi(xp))
        xb = _bf16(_unpack_lo(xp))
        g = _dot(xa, wg_s[0]) + _dot(xb, wg_s[1])
        u = _dot(xa, wu_s[0]) + _dot(xb, wu_s[1])
        h = _bf16(g * _sigmoid(g) * u)
        y = _dot(h, wd_s[...])
        o_ref[...] = _pack_bf16_pair(y[:, :half], y[:, half:])

    @pl.when(b >= nu_ref[0])
    def _():
        o_ref[...] = jnp.zeros_like(o_ref)


def moe_experts(block_expert, n_used, xs, w_gate, w_up, w_down, bm):
    n_rows, half = xs.shape
    ne, d, f = w_gate.shape
    nblk = n_rows // bm
    grid_spec = pltpu.PrefetchScalarGridSpec(
        num_scalar_prefetch=2, grid=(nblk,),
        in_specs=[
            pl.BlockSpec((bm, half), lambda b, be, nu: (jnp.minimum(b, nu[0] - 1), 0)),
            pl.BlockSpec((None, d, f), lambda b, be, nu: (be[b], 0, 0)),
            pl.BlockSpec((None, d, f), lambda b, be, nu: (be[b], 0, 0)),
            pl.BlockSpec((None, f, d), lambda b, be, nu: (be[b], 0, 0)),
        ],
        out_specs=pl.BlockSpec((bm, half), lambda b, be, nu: (b, 0)),
        scratch_shapes=[pltpu.VMEM((2, half, f), jnp.bfloat16), pltpu.VMEM((2, half, f), jnp.bfloat16),
                        pltpu.VMEM((f, d), jnp.bfloat16)],
    )
    return pl.pallas_call(
        _expert_kernel, grid_spec=grid_spec,
        out_shape=jax.ShapeDtypeStruct((n_rows, half), jnp.uint32),
        compiler_params=_cp(("arbitrary",)), name="moe_experts",
    )(block_expert, n_used, xs, w_gate, w_up, w_down)


def _combine_kernel(dest_ref, ys_ref, gk_ref, sh_ref, x_ref, ga_ref, o_ref, buf, sem, *, tt):
    def body(t, carry):
        for k in range(TOP_K):
            pltpu.make_async_copy(ys_ref.at[pl.ds(dest_ref[k, t], 1)], buf.at[k, pl.ds(t, 1)], sem).start()
        return carry

    lax.fori_loop(0, tt, body, 0)
    for k in range(TOP_K):
        pltpu.make_async_copy(ys_ref.at[pl.ds(0, tt)], buf.at[k], sem).wait()
    half = buf.shape[2]
    acc_a = jnp.zeros((tt, half), jnp.float32)
    acc_b = jnp.zeros((tt, half), jnp.float32)
    for k in range(TOP_K):
        p = buf[k]
        gk = gk_ref[:, k:k + 1]
        acc_a = acc_a + gk * _unpack_hi(p)
        acc_b = acc_b + gk * _unpack_lo(p)
    sh = sh_ref[...].astype(jnp.float32)
    o_ref[:, 0:half] = x_ref[:, 0:half] + ga_ref[:, 0:half] * (acc_a + sh[:, 0:half])
    o_ref[:, half:] = x_ref[:, half:] + ga_ref[:, half:] * (acc_b + sh[:, half:])


def moe_combine(dest, ys, gates_tk, shared, x2d, gate_ada, seq, tt=128):
    t, d = x2d.shape
    half = ys.shape[1]
    tt = min(tt, seq)
    return pl.pallas_call(
        functools.partial(_combine_kernel, tt=tt), grid=(t // tt,),
        in_specs=[
            pl.BlockSpec((TOP_K, tt), lambda i: (0, i), memory_space=pltpu.SMEM),
            pl.BlockSpec(memory_space=pl.ANY),
            pl.BlockSpec((tt, TOP_K), lambda i: (i, 0)),
            pl.BlockSpec((tt, d), lambda i: (i, 0)),
            pl.BlockSpec((tt, d), lambda i: (i, 0)),
            pl.BlockSpec((None, 1, d), lambda i: ((i * tt) // seq, 0, 0)),
        ],
        out_specs=pl.BlockSpec((tt, d), lambda i: (i, 0)),
        out_shape=jax.ShapeDtypeStruct((t, d), jnp.float32),
        scratch_shapes=[pltpu.VMEM((TOP_K, tt, half), jnp.uint32), pltpu.SemaphoreType.DMA(())],
        compiler_params=_cp(("arbitrary",)), name="moe_combine",
    )(dest, ys, gates_tk, shared, x2d, gate_ada)


MOE_ROWS = 256


def moe_layer(x2d, hn, hp, gate_ada, seq, router_w, router_bias, w_gate, w_up, w_down, s_gate, s_up, s_down):
    t, d = x2d.shape
    ne = router_w.shape[1]
    bm = min(MOE_ROWS, t)
    gates, sel, pos, cnt = moe_router(hn, router_w, router_bias)
    counts = cnt[:, 0].astype(jnp.int32)
    padded = (counts + bm - 1) // bm * bm
    pad_end = jnp.cumsum(padded)
    pad_start = pad_end - padded
    n_rows = t * TOP_K + ne * bm
    nblk = n_rows // bm
    dest_dense = pad_start[:, None] + pos.astype(jnp.int32)
    selb = sel > 0.0
    slot = jnp.cumsum(sel, axis=0) - sel
    ks = jnp.arange(TOP_K, dtype=jnp.float32)[:, None, None]
    pick = selb[None] & (slot[None] == ks)
    dest = jnp.sum(jnp.where(pick, dest_dense[None], 0), axis=1).astype(jnp.int32)
    gates_k = jnp.sum(jnp.where(pick, gates[None], 0.0), axis=1)
    block_expert = jnp.minimum(
        jnp.searchsorted(pad_end, jnp.arange(nblk, dtype=jnp.int32) * bm, side='right'), ne - 1).astype(jnp.int32)
    n_used = (pad_end[-1:] // bm).astype(jnp.int32)
    tail = (n_used[0] + jnp.arange(ne, dtype=jnp.int32)) * bm
    zero_blocks = jnp.concatenate([jnp.where(padded > counts, pad_end - bm, -1),
                                   jnp.where(tail < n_rows, tail, -1)]).astype(jnp.int32)
    xs = moe_dispatch(zero_blocks, dest, hp, n_rows, bm)
    ys = moe_experts(block_expert, n_used, xs, w_gate, w_up, w_down, bm)
    hs = matmul_swiglu(hn, _bf16(s_gate), _bf16(s_up))
    shared = matmul(hs, _bf16(s_down), jnp.bfloat16)
    return moe_combine(dest, ys, gates_k.T, shared, x2d, gate_ada, seq)


def kernel(x, c, mix_norm_g, mix_ada_w, mix_ada_b, ev_w_in, ev_conv_w, ev_conv_b, ev_ln_g, ev_ln_b, ev_q_norm_g, ev_k_norm_g, ev_w_out, od_w_in, od_log_dt, od_lambda_re, od_lambda_im, od_b_re, od_b_im, od_c_re, od_c_im, od_d, od_w_out, ffn_norm_g, ffn_ada_w, ffn_ada_b, router_w, router_bias, exp_gate, exp_up, exp_down, sh_gate, sh_up, sh_down):
    bsz, seq, d = x.shape
    depth = mix_norm_g.shape[0]
    t = bsz * seq
    c_pad = jnp.zeros((8, d), jnp.float32).at[:bsz].set(c)
    mix_mod = adaln_all(c_pad, mix_ada_w, mix_ada_b)[:, :bsz]
    ffn_mod = adaln_all(c_pad, ffn_ada_w, ffn_ada_b)[:, :bsz]

    def split(mod):
        return tuple(mod[:, None, k * d:(k + 1) * d] for k in range(3))

    for i in range(depth):
        j = i // 2
        shift, scale, gate = split(mix_mod[i])
        hn = norm_modulate(x, mix_norm_g[i], scale, shift)
        x2d = x.reshape(t, d)
        if i % 2 == 0:
            cc = ev_conv_w.shape[-1]
            dh = ev_q_norm_g.shape[-1]
            sbw = (ev_w_in.shape[-1] - 2 * cc) // 3
            proj = matmul(hn, _bf16(ev_w_in[j]), jnp.bfloat16)
            a = conv_branch(proj, ev_conv_w[j], ev_conv_b[j], ev_ln_g[j], ev_ln_b[j], seq)
            qn, kn = qk_norm(proj, ev_q_norm_g[j], ev_k_norm_g[j], 2 * cc, sbw)
            o = stickbreak_attention(qn, kn, proj, 2 * cc + 2 * sbw, bsz, seq, dh)
            x2d = matmul2_resid(a, o, _bf16(ev_w_out[j]), x2d, gate, seq)
        else:
            u = matmul(hn, _bf16(od_w_in[j]), jnp.float32)
            wts = s5_weights(od_log_dt[j], od_lambda_re[j], od_lambda_im[j], od_b_re[j], od_b_im[j],
                             od_c_re[j], od_c_im[j], od_d[j])
            y = s5_scan(u.reshape(bsz, seq, -1), *wts)
            x2d = matmul_glu_resid(y.reshape(t, -1), _bf16(od_w_out[j]), x2d, gate, seq)
        x = x2d.reshape(bsz, seq, d)
        shift, scale, gate = split(ffn_mod[i])
        hn, hp = norm_modulate(x, ffn_norm_g[i], scale, shift, packed=True)
        x2d = moe_layer(x.reshape(t, d), hn, hp, gate, seq, router_w[i], router_bias[i], exp_gate[i], exp_up[i],
                        exp_down[i], sh_gate[i], sh_up[i], sh_down[i])
        x = x2d.reshape(bsz, seq, d)
    return x
```

```python
import functools
import math

import jax
import jax.numpy as jnp
from jax import lax
from jax.experimental import pallas as pl
from jax.experimental.pallas import tpu as pltpu

EPS = 1e-6
TOP_K = 8
N_GROUPS = 8
TOPK_GROUPS = 4
ROUTED_SCALE = 2.5
S5_CHUNK = 8
S5_BLOCK_LANES = 128

V7X_VMEM_LIMIT = 56 * 1024 * 1024


def _cp(dims, vmem=V7X_VMEM_LIMIT):
    return pltpu.CompilerParams(dimension_semantics=dims, vmem_limit_bytes=vmem)


def _sigmoid(x):
    return 1.0 / (1.0 + jnp.exp(-x))


def _bf16(x):
    return x.astype(jnp.bfloat16)


def _dot(a, b):
    return jnp.dot(a, b, preferred_element_type=jnp.float32)


def _tile(n, target, quantum=128):
    if n <= target:
        return n
    best = quantum
    for cand in range(quantum, target + 1, quantum):
        if n % cand == 0:
            best = cand
    assert n % best == 0, (n, target)
    return best


def _dot_nt(a, b):
    return lax.dot_general(a, b, (((1,), (1,)), ((), ())), preferred_element_type=jnp.float32)


def _adaln_kernel(c_ref, w_ref, b_ref, o_ref):
    c = c_ref[...]
    sc = _bf16(c * _sigmoid(c))
    o_ref[...] = _dot(sc, _bf16(w_ref[...])) + b_ref[...]


def adaln_all(c_pad, w, b, tn=512):
    nl, d, n = w.shape
    tn = min(tn, n)
    return pl.pallas_call(
        _adaln_kernel,
        grid=(nl, n // tn),
        in_specs=[
            pl.BlockSpec((8, d), lambda l, j: (0, 0)),
            pl.BlockSpec((None, d, tn), lambda l, j: (l, 0, j)),
            pl.BlockSpec((None, 1, tn), lambda l, j: (l, 0, j)),
        ],
        out_specs=pl.BlockSpec((None, 8, tn), lambda l, j: (l, 0, j)),
        out_shape=jax.ShapeDtypeStruct((nl, 8, n), jnp.float32),
        compiler_params=_cp(("parallel", "parallel")),
        name="adaln",
    )(c_pad, w, b.reshape(nl, 1, n))


def _pack_bf16_pair(a, b):
    ua = pltpu.bitcast(_bf16(a).astype(jnp.float32), jnp.uint32)
    ub = pltpu.bitcast(_bf16(b).astype(jnp.float32), jnp.uint32)
    return (ua & jnp.uint32(0xFFFF0000)) | (ub >> 16)


def _unpack_hi(p):
    return pltpu.bitcast(p & jnp.uint32(0xFFFF0000), jnp.float32)


def _unpack_lo(p):
    return pltpu.bitcast(p << 16, jnp.float32)


def _norm_mod(x, g_ref, sc_ref, sh_ref):
    xf = x.astype(jnp.float32)
    ms = jnp.mean(xf * xf, axis=-1, keepdims=True)
    return xf * lax.rsqrt(ms + EPS) * g_ref[...] * (1.0 + sc_ref[...]) + sh_ref[...]


def _norm_kernel(x_ref, g_ref, sc_ref, sh_ref, o_ref):
    o_ref[...] = _bf16(_norm_mod(x_ref[...], g_ref, sc_ref, sh_ref))


def _norm_pack_kernel(x_ref, g_ref, sc_ref, sh_ref, o_ref, p_ref):
    y = _norm_mod(x_ref[...], g_ref, sc_ref, sh_ref)
    o_ref[...] = _bf16(y)
    h = y.shape[-1] // 2
    p_ref[...] = _pack_bf16_pair(y[:, :h], y[:, h:])


def norm_modulate(x, g, scale, shift, packed=False, ts=256):
    b, s, d = x.shape
    ts = min(ts, s)
    nt = s // ts
    in_specs = [
        pl.BlockSpec((None, ts, d), lambda i, j: (i, j, 0)),
        pl.BlockSpec((1, d), lambda i, j: (0, 0)),
        pl.BlockSpec((None, 1, d), lambda i, j: (i, 0, 0)),
        pl.BlockSpec((None, 1, d), lambda i, j: (i, 0, 0)),
    ]
    o_spec = pl.BlockSpec((ts, d), lambda i, j: (i * nt + j, 0))
    o_shape = jax.ShapeDtypeStruct((b * s, d), jnp.bfloat16)
    if not packed:
        return pl.pallas_call(
            _norm_kernel, grid=(b, nt), in_specs=in_specs, out_specs=o_spec, out_shape=o_shape,
            compiler_params=_cp(("parallel", "parallel")), name="norm_mod",
        )(x, g.reshape(1, d), scale, shift)
    return pl.pallas_call(
        _norm_pack_kernel, grid=(b, nt), in_specs=in_specs,
        out_specs=[o_spec, pl.BlockSpec((ts, d // 2), lambda i, j: (i * nt + j, 0))],
        out_shape=[o_shape, jax.ShapeDtypeStruct((b * s, d // 2), jnp.uint32)],
        compiler_params=_cp(("parallel", "parallel")), name="norm_mod_pack",
    )(x, g.reshape(1, d), scale, shift)


def _mm_kernel(a_ref, w_ref, o_ref):
    o_ref[...] = _dot(_bf16(a_ref[...]), w_ref[...]).astype(o_ref.dtype)


def matmul(a, w, out_dtype, bm=1024, bn=1024):
    m, k = a.shape
    n = w.shape[1]
    bm, bn = _tile(m, bm), _tile(n, bn)
    return pl.pallas_call(
        _mm_kernel, grid=(m // bm, n // bn),
        in_specs=[pl.BlockSpec((bm, k), lambda i, j: (i, 0)), pl.BlockSpec((k, bn), lambda i, j: (0, j))],
        out_specs=pl.BlockSpec((bm, bn), lambda i, j: (i, j)),
        out_shape=jax.ShapeDtypeStruct((m, n), out_dtype),
        compiler_params=_cp(("parallel", "parallel")), name="matmul",
    )(a, w)


def _mm2_resid_kernel(a1_ref, a2_ref, w1_ref, w2_ref, x_ref, g_ref, o_ref):
    m = _dot(a1_ref[...], w1_ref[...]) + _dot(a2_ref[...], w2_ref[...])
    o_ref[...] = x_ref[...] + g_ref[...] * m


def matmul2_resid(a1, a2, w, x2d, gate, seq, bm=1024, bn=1024):
    m, k1 = a1.shape
    k2 = a2.shape[1]
    assert k1 == k2
    n = w.shape[1]
    bm, bn = _tile(seq, bm), _tile(n, bn)
    return pl.pallas_call(
        _mm2_resid_kernel, grid=(m // bm, n // bn),
        in_specs=[
            pl.BlockSpec((bm, k1), lambda i, j: (i, 0)),
            pl.BlockSpec((bm, k2), lambda i, j: (i, 0)),
            pl.BlockSpec((k1, bn), lambda i, j: (0, j)),
            pl.BlockSpec((k2, bn), lambda i, j: (1, j)),
            pl.BlockSpec((bm, bn), lambda i, j: (i, j)),
            pl.BlockSpec((None, 1, bn), lambda i, j: ((i * bm) // seq, 0, j)),
        ],
        out_specs=pl.BlockSpec((bm, bn), lambda i, j: (i, j)),
        out_shape=jax.ShapeDtypeStruct((m, n), jnp.float32),
        compiler_params=_cp(("parallel", "parallel")), name="matmul2_resid",
    )(a1, a2, w, w, x2d, gate)


def _mm_glu_resid_kernel(a_ref, wa_ref, wb_ref, x_ref, g_ref, o_ref):
    a = _bf16(a_ref[...])
    va = _dot(a, wa_ref[...])
    vb = _dot(a, wb_ref[...])
    o_ref[...] = x_ref[...] + g_ref[...] * (va * _sigmoid(vb))


def matmul_glu_resid(a, w, x2d, gate, seq, bm=512, bn=512):
    m, k = a.shape
    n = w.shape[1] // 2
    bm, bn = _tile(seq, bm), _tile(n, bn)
    nb = n // bn
    return pl.pallas_call(
        _mm_glu_resid_kernel, grid=(m // bm, nb),
        in_specs=[
            pl.BlockSpec((bm, k), lambda i, j: (i, 0)),
            pl.BlockSpec((k, bn), lambda i, j: (0, j)),
            pl.BlockSpec((k, bn), lambda i, j: (0, j + nb)),
            pl.BlockSpec((bm, bn), lambda i, j: (i, j)),
            pl.BlockSpec((None, 1, bn), lambda i, j: ((i * bm) // seq, 0, j)),
        ],
        out_specs=pl.BlockSpec((bm, bn), lambda i, j: (i, j)),
        out_shape=jax.ShapeDtypeStruct((m, n), jnp.float32),
        compiler_params=_cp(("parallel", "parallel")), name="matmul_glu_resid",
    )(a, w, w, x2d, gate)


def _mm_swiglu_kernel(a_ref, wg_ref, wu_ref, o_ref):
    a = a_ref[...]
    g = _dot(a, wg_ref[...])
    u = _dot(a, wu_ref[...])
    o_ref[...] = _bf16(g * _sigmoid(g) * u)


def matmul_swiglu(a, wg, wu, bm=1024, bn=512):
    m, k = a.shape
    n = wg.shape[1]
    bm, bn = _tile(m, bm), _tile(n, bn)
    return pl.pallas_call(
        _mm_swiglu_kernel, grid=(m // bm, n // bn),
        in_specs=[
            pl.BlockSpec((bm, k), lambda i, j: (i, 0)),
            pl.BlockSpec((k, bn), lambda i, j: (0, j)),
            pl.BlockSpec((k, bn), lambda i, j: (0, j)),
        ],
        out_specs=pl.BlockSpec((bm, bn), lambda i, j: (i, j)),
        out_shape=jax.ShapeDtypeStruct((m, n), jnp.bfloat16),
        compiler_params=_cp(("parallel", "parallel")), name="matmul_swiglu",
    )(a, wg, wu)


CONV_HALO = 32
CONV_ROWS = 64
CONV_LANES = 256


def _conv_kernel(v_ref, g_ref, vh_ref, gh_ref, w_ref, cb_ref, lg_ref, lb_ref, o_ref, abuf, cbuf, *,
                 tiles_per_seq, width):
    ts, cc = o_ref.shape
    first = (pl.program_id(0) % tiles_per_seq) == 0
    vh = vh_ref[...].astype(jnp.float32)
    gh = gh_ref[...].astype(jnp.float32)
    abuf[0:CONV_HALO, :] = jnp.where(first, 0.0, vh * _sigmoid(gh))
    v = v_ref[...].astype(jnp.float32)
    g = g_ref[...].astype(jnp.float32)
    abuf[CONV_HALO:, :] = v * _sigmoid(g)
    off = CONV_HALO - (width - 1)
    for lc in range(cc // CONV_LANES):
        ls = slice(lc * CONV_LANES, (lc + 1) * CONV_LANES)
        for rc in range(ts // CONV_ROWS):
            r0 = rc * CONV_ROWS
            acc = jnp.zeros((CONV_ROWS, CONV_LANES), jnp.float32)
            for j in range(width):
                acc = acc + w_ref[j:j + 1, ls] * abuf[r0 + off + j:r0 + off + j + CONV_ROWS, ls]
            cbuf[r0:r0 + CONV_ROWS, ls] = acc + cb_ref[:, ls]
    c = cbuf[...]
    mu = jnp.mean(c, axis=-1, keepdims=True)
    cen = c - mu
    var = jnp.mean(cen * cen, axis=-1, keepdims=True)
    y = cen * lax.rsqrt(var + EPS) * lg_ref[...] + lb_ref[...]
    o_ref[...] = _bf16(y * _sigmoid(y))


def conv_branch(proj, conv_w, conv_b, ln_g, ln_b, seq, ts=128):
    t = proj.shape[0]
    width, cc = conv_w.shape
    ts = min(ts, seq)
    assert width - 1 <= CONV_HALO and ts % CONV_HALO == 0 and cc % CONV_LANES == 0 and ts % CONV_ROWS == 0
    r = ts // CONV_HALO
    halo_idx = lambda i: jnp.maximum(i * r - 1, 0)
    kern = functools.partial(_conv_kernel, tiles_per_seq=seq // ts, width=width)
    return pl.pallas_call(
        kern, grid=(t // ts,),
        in_specs=[
            pl.BlockSpec((ts, cc), lambda i: (i, 0)),
            pl.BlockSpec((ts, cc), lambda i: (i, 1)),
            pl.BlockSpec((CONV_HALO, cc), lambda i: (halo_idx(i), 0)),
            pl.BlockSpec((CONV_HALO, cc), lambda i: (halo_idx(i), 1)),
            pl.BlockSpec((width, cc), lambda i: (0, 0)),
            pl.BlockSpec((1, cc), lambda i: (0, 0)),
            pl.BlockSpec((1, cc), lambda i: (0, 0)),
            pl.BlockSpec((1, cc), lambda i: (0, 0)),
        ],
        out_specs=pl.BlockSpec((ts, cc), lambda i: (i, 0)),
        out_shape=jax.ShapeDtypeStruct((t, cc), jnp.bfloat16),
        scratch_shapes=[pltpu.VMEM((CONV_HALO + ts, cc), jnp.float32), pltpu.VMEM((ts, cc), jnp.float32)],
        compiler_params=_cp(("parallel",)), name="conv_branch",
    )(proj, proj, proj, proj, conv_w, conv_b.reshape(1, cc), ln_g.reshape(1, cc), ln_b.reshape(1, cc))


def _qknorm_kernel(q_ref, k_ref, gq_ref, gk_ref, qo_ref, ko_ref, *, dh):
    inv_sqrt = 1.0 / math.sqrt(dh)
    for src, g_ref, dst, mul in ((q_ref, gq_ref, qo_ref, inv_sqrt), (k_ref, gk_ref, ko_ref, 1.0)):
        for h in range(src.shape[1] // dh):
            ls = slice(h * dh, (h + 1) * dh)
            xf = src[:, ls].astype(jnp.float32)
            ms = jnp.mean(xf * xf, axis=-1, keepdims=True)
            dst[:, ls] = _bf16(xf * lax.rsqrt(ms + EPS) * g_ref[...] * mul)


def qk_norm(proj, gq, gk, col0, width, ts=512):
    t = proj.shape[0]
    dh = gq.shape[0]
    ts = min(ts, t)
    qb, kb = col0 // width, col0 // width + 1
    return pl.pallas_call(
        functools.partial(_qknorm_kernel, dh=dh), grid=(t // ts,),
        in_specs=[
            pl.BlockSpec((ts, width), lambda i: (i, qb)),
            pl.BlockSpec((ts, width), lambda i: (i, kb)),
            pl.BlockSpec((1, dh), lambda i: (0, 0)),
            pl.BlockSpec((1, dh), lambda i: (0, 0)),
        ],
        out_specs=[pl.BlockSpec((ts, width), lambda i: (i, 0)), pl.BlockSpec((ts, width), lambda i: (i, 0))],
        out_shape=[jax.ShapeDtypeStruct((t, width), jnp.bfloat16)] * 2,
        compiler_params=_cp(("parallel",)), name="qk_norm",
    )(proj, proj, gq.reshape(1, dh), gk.reshape(1, dh))


ATT_TILE = 1024
ATT_SUB = 256


def _attn_kernel(qi_ref, kj_ref, q_ref, k_ref, v_ref, u_ref, o_ref, acc_ref, r_ref):
    p = pl.program_id(2)
    qi = qi_ref[p]
    kj = kj_ref[p]
    tq = q_ref.shape[0]
    tk = k_ref.shape[0]
    sub = u_ref.shape[1]

    @pl.when(kj == qi)
    def _():
        acc_ref[...] = jnp.zeros_like(acc_ref)
        r_ref[...] = jnp.zeros_like(r_ref)

    def process(diagonal):
        nsb = tk // sub
        z = _dot_nt(q_ref[...], k_ref[...])
        lk = -(jnp.maximum(z, 0.0) + jnp.log(1.0 + jnp.exp(-jnp.abs(z))))
        if diagonal:
            row = lax.broadcasted_iota(jnp.int32, (tq, tk), 0)
            col = lax.broadcasted_iota(jnp.int32, (tq, tk), 1)
            valid = col < row
            lm = jnp.where(valid, lk, 0.0)
        else:
            lm = lk
        hi = _bf16(lm)
        lo = _bf16(lm - hi.astype(jnp.float32))
        hl = jnp.concatenate(
            [jnp.concatenate([hi[:, s * sub:(s + 1) * sub], lo[:, s * sub:(s + 1) * sub]], axis=1)
             for s in range(nsb)], axis=0)
        cs = _dot(hl, u_ref[...])
        r = r_ref[...]
        later = [None] * nsb
        for s in reversed(range(nsb)):
            later[s] = cs[s * tq:(s + 1) * tq, :] + r
            r = r + jnp.sum(lm[:, s * sub:(s + 1) * sub], axis=1, keepdims=True)
        r_ref[...] = r
        a = jnp.exp(z + lk + jnp.concatenate(later, axis=1))
        if diagonal:
            a = jnp.where(valid, a, 0.0)
        acc_ref[...] += _dot(_bf16(a), v_ref[...])

    @pl.when(kj == qi)
    def _():
        process(True)

    @pl.when(kj != qi)
    def _():
        process(False)

    @pl.when(kj == 0)
    def _():
        o_ref[...] = acc_ref[...].astype(o_ref.dtype)


def stickbreak_attention(qn, kn, proj, vcol0, batch, seq, dh):
    t, w = qn.shape
    heads = w // dh
    tile = min(ATT_TILE, seq)
    sub = min(ATT_SUB, tile)
    nq = seq // tile
    pairs = [(i, j) for i in range(nq) for j in range(i, -1, -1)]
    qi = jnp.asarray([p[0] for p in pairs], jnp.int32)
    kj = jnp.asarray([p[1] for p in pairs], jnp.int32)
    r = lax.broadcasted_iota(jnp.int32, (sub, sub), 0)
    c = lax.broadcasted_iota(jnp.int32, (sub, sub), 1)
    tri = _bf16((r > c).astype(jnp.float32))
    u2 = jnp.concatenate([tri, tri], axis=0)
    vb = vcol0 // dh
    grid_spec = pltpu.PrefetchScalarGridSpec(
        num_scalar_prefetch=2,
        grid=(batch, heads, len(pairs)),
        in_specs=[
            pl.BlockSpec((tile, dh), lambda b, h, p, qi, kj: (b * nq + qi[p], h)),
            pl.BlockSpec((tile, dh), lambda b, h, p, qi, kj: (b * nq + kj[p], h)),
            pl.BlockSpec((tile, dh), lambda b, h, p, qi, kj: (b * nq + kj[p], vb + h)),
            pl.BlockSpec((2 * sub, sub), lambda b, h, p, qi, kj: (0, 0)),
        ],
        out_specs=pl.BlockSpec((tile, dh), lambda b, h, p, qi, kj: (b * nq + qi[p], h)),
        scratch_shapes=[pltpu.VMEM((tile, dh), jnp.float32), pltpu.VMEM((tile, 1), jnp.float32)],
    )
    return pl.pallas_call(
        _attn_kernel, grid_spec=grid_spec,
        out_shape=jax.ShapeDtypeStruct((t, w), jnp.bfloat16),
        compiler_params=_cp(("parallel", "parallel", "arbitrary")), name="stickbreak_attn",
    )(qi, kj, qn, kn, proj, u2)


def _gelu_tanh(x):
    return 0.5 * x * (1.0 + jnp.tanh(math.sqrt(2.0 / math.pi) * (x + 0.044715 * (x * x * x))))


def _s5_kernel(u_ref, wm_ref, we_ref, wc_ref, lam_ref, y_ref, s_ref, e_ref, sp_ref):
    tt, bl = u_ref.shape
    nch = tt // S5_CHUNK
    half = s_ref.shape[1]

    @pl.when(pl.program_id(2) == 0)
    def _():
        s_ref[...] = jnp.zeros_like(s_ref)

    ucat = _bf16(jnp.concatenate(
        [u_ref[pl.ds(j, nch, stride=S5_CHUNK), :] for j in range(S5_CHUNK)], axis=1))
    e_ref[...] = _dot(ucat, we_ref[...])
    lam_re = lam_ref[0:1, :]
    lam_im = lam_ref[1:2, :]

    def body(c, carry):
        sre, sim = carry
        sp_ref[pl.ds(c, 1), 0:half] = sre
        sp_ref[pl.ds(c, 1), half:] = sim
        ere = e_ref[pl.ds(c, 1), 0:half]
        eim = e_ref[pl.ds(c, 1), half:]
        return (lam_re * sre - lam_im * sim + ere, lam_re * sim + lam_im * sre + eim)

    sre, sim = lax.fori_loop(0, nch, body, (s_ref[0:1, :], s_ref[1:2, :]))
    s_ref[0:1, :] = sre
    s_ref[1:2, :] = sim
    ycat = _dot(ucat, wm_ref[...]) + _dot(_bf16(sp_ref[...]), wc_ref[...])
    for j in range(S5_CHUNK):
        y_ref[pl.ds(j, nch, stride=S5_CHUNK), :] = _gelu_tanh(ycat[:, j * bl:(j + 1) * bl])


def s5_weights(log_dt, lam_re, lam_im, b_re, b_im, c_re, c_im, d_skip):
    g, p = lam_re.shape
    hg = b_re.shape[-1]
    L = S5_CHUNK
    gb = S5_BLOCK_LANES // hg
    nb = g // gb
    dt = jnp.exp(log_dt)[:, None]
    mag = jnp.exp(lam_re * dt)
    ang = lam_im * dt
    lbar_re = mag * jnp.cos(ang)
    lbar_im = mag * jnp.sin(ang)
    den = lam_re * lam_re + lam_im * lam_im
    nr = lbar_re - 1.0
    f_re = (nr * lam_re + lbar_im * lam_im) / den
    f_im = (lbar_im * lam_re - nr * lam_im) / den
    bbar_re = f_re[..., None] * b_re - f_im[..., None] * b_im
    bbar_im = f_re[..., None] * b_im + f_im[..., None] * b_re
    n = jnp.arange(L + 1, dtype=jnp.float32)[:, None, None]
    pw_mag = jnp.exp(n * (lam_re * dt)[None])
    pw_re = pw_mag * jnp.cos(n * ang[None])
    pw_im = pw_mag * jnp.sin(n * ang[None])
    lb_re = pw_re[..., None] * bbar_re[None] - pw_im[..., None] * bbar_im[None]
    lb_im = pw_re[..., None] * bbar_im[None] + pw_im[..., None] * bbar_re[None]
    ktau = (jnp.einsum('gop,ngpi->ngoi', c_re, lb_re[:L], precision='highest')
            - jnp.einsum('gop,ngpi->ngoi', c_im, lb_im[:L], precision='highest'))
    jj = jnp.arange(L)[:, None]
    tt = jnp.arange(L)[None, :]
    tau = jnp.clip(tt - jj, 0, L - 1)
    kjt = jnp.where((tt >= jj)[..., None, None, None], ktau[tau], 0.0)
    kjt = kjt + (jj == tt)[..., None, None, None] * (d_skip[:, :, None] * jnp.eye(hg))[None, None]
    eye = jnp.eye(gb, dtype=jnp.float32)
    kb = kjt.reshape(L, L, nb, gb, hg, hg)
    w_main = jnp.einsum('jtbgoi,gh->bjgitho', kb, eye).reshape(nb, L * gb * hg, L * gb * hg)
    le_re = lb_re[L - 1 - jnp.arange(L)].reshape(L, nb, gb, p, hg)
    le_im = lb_im[L - 1 - jnp.arange(L)].reshape(L, nb, gb, p, hg)
    le = jnp.stack([le_re, le_im], axis=0)
    w_end = jnp.einsum('qjbgpi,gh->bjgiqhp', le, eye).reshape(nb, L * gb * hg, 2 * gb * p)
    m_re = c_re[None] * pw_re[1:, :, None, :] - c_im[None] * pw_im[1:, :, None, :]
    m_im = c_re[None] * pw_im[1:, :, None, :] + c_im[None] * pw_re[1:, :, None, :]
    mc = jnp.stack([m_re, -m_im], axis=0).reshape(2, L, nb, gb, hg, p)
    w_carry = jnp.einsum('qtbgop,gh->bqgptho', mc, eye).reshape(nb, 2 * gb * p, L * gb * hg)
    lam_l = jnp.stack([pw_re[L].reshape(nb, gb * p), pw_im[L].reshape(nb, gb * p)], axis=1)
    return _bf16(w_main), _bf16(w_end), _bf16(w_carry), lam_l


def s5_scan(u, w_main, w_end, w_carry, lam_l, tt=2048):
    b, s, w = u.shape
    nb, kdim, sdim = w_end.shape
    bl = S5_BLOCK_LANES
    tt = min(tt, s)
    nch = tt // S5_CHUNK
    return pl.pallas_call(
        _s5_kernel, grid=(b, nb, s // tt),
        in_specs=[
            pl.BlockSpec((None, tt, bl), lambda i, j, t: (i, t, j)),
            pl.BlockSpec((None, kdim, kdim), lambda i, j, t: (j, 0, 0)),
            pl.BlockSpec((None, kdim, sdim), lambda i, j, t: (j, 0, 0)),
            pl.BlockSpec((None, sdim, kdim), lambda i, j, t: (j, 0, 0)),
            pl.BlockSpec((None, 2, sdim // 2), lambda i, j, t: (j, 0, 0)),
        ],
        out_specs=pl.BlockSpec((None, tt, bl), lambda i, j, t: (i, t, j)),
        out_shape=jax.ShapeDtypeStruct((b, s, w), jnp.float32),
        scratch_shapes=[pltpu.VMEM((2, sdim // 2), jnp.float32), pltpu.VMEM((nch, sdim), jnp.float32),
                        pltpu.VMEM((nch, sdim), jnp.float32)],
        compiler_params=_cp(("parallel", "parallel", "arbitrary")), name="s5_scan",
    )(u, w_main, w_end, w_carry, lam_l)


def _router_kernel(h_ref, rw_ref, bias_ref, ut_ref, lt_ref, eid_ref, posk_ref, gatek_ref, cnt_ref, run_ref):
    ne = rw_ref.shape[0]
    tt = h_ref.shape[0]
    per = ne // N_GROUPS

    @pl.when(pl.program_id(0) == 0)
    def _():
        run_ref[...] = jnp.zeros_like(run_ref)

    s = _sigmoid(_dot_nt(rw_ref[...], h_ref[...]))
    bsc = s + bias_ref[...]
    b3 = bsc.reshape(N_GROUPS, per, tt)
    m1 = jnp.max(b3, axis=1, keepdims=True)
    is1 = b3 == m1
    n1 = jnp.sum(is1.astype(jnp.float32), axis=1, keepdims=True)
    m2 = jnp.max(jnp.where(is1, -jnp.inf, b3), axis=1, keepdims=True)
    gs = (m1 + jnp.where(n1 >= 2.0, m1, m2)).reshape(N_GROUPS, tt)
    gidx = lax.broadcasted_iota(jnp.int32, (N_GROUPS, tt), 0)
    grank = jnp.zeros((N_GROUPS, tt), jnp.float32)
    for g in range(N_GROUPS):
        row = gs[g:g + 1, :]
        ahead = (row > gs) | ((row == gs) & (g < gidx))
        grank = grank + ahead.astype(jnp.float32)
    gok = (grank < float(TOPK_GROUPS)).astype(jnp.float32)
    eok = jnp.broadcast_to(gok.reshape(N_GROUPS, 1, tt), (N_GROUPS, per, tt)).reshape(ne, tt)
    masked = jnp.where(eok > 0.0, bsc, -jnp.inf)
    eidx = lax.broadcasted_iota(jnp.int32, (ne, tt), 0)
    erank = jnp.zeros((ne, tt), jnp.float32)
    for e in range(ne):
        row = masked[e:e + 1, :]
        ahead = (row > masked) | ((row == masked) & (e < eidx))
        erank = erank + ahead.astype(jnp.float32)
    sel = erank < float(TOP_K)
    self32 = sel.astype(jnp.float32)
    selb = _bf16(self32)
    gsel = jnp.where(sel, s, 0.0)
    gsum = jnp.sum(gsel, axis=0, keepdims=True)
    gate = gsel / gsum * ROUTED_SCALE
    pos = _dot(selb, ut_ref[...]) + run_ref[...]
    run_ref[...] += jnp.sum(self32, axis=1, keepdims=True)
    cnt_ref[...] = jnp.broadcast_to(run_ref[...], cnt_ref.shape)
    slot = _dot(lt_ref[...], selb)
    eidf = eidx.astype(jnp.float32)
    for k in range(TOP_K):
        m = sel & (slot == float(k))
        eid_ref[k:k + 1, :] = jnp.sum(jnp.where(m, eidf, 0.0), axis=0, keepdims=True).astype(jnp.int32)
        posk_ref[k:k + 1, :] = jnp.sum(jnp.where(m, pos, 0.0), axis=0, keepdims=True).astype(jnp.int32)
        gatek_ref[k:k + 1, :] = jnp.sum(jnp.where(m, gate, 0.0), axis=0, keepdims=True)


def moe_router(hn, router_w, router_bias, tt=512):
    t, d = hn.shape
    ne = router_w.shape[1]
    tt = min(tt, t)
    r = lax.broadcasted_iota(jnp.int32, (tt, tt), 0)
    c = lax.broadcasted_iota(jnp.int32, (tt, tt), 1)
    ut = _bf16((r < c).astype(jnp.float32))
    re = lax.broadcasted_iota(jnp.int32, (ne, ne), 0)
    ce = lax.broadcasted_iota(jnp.int32, (ne, ne), 1)
    lt = _bf16((ce < re).astype(jnp.float32))
    slot_spec = pl.BlockSpec((TOP_K, tt), lambda i: (0, i))
    return pl.pallas_call(
        _router_kernel, grid=(t // tt,),
        in_specs=[
            pl.BlockSpec((tt, d), lambda i: (i, 0)),
            pl.BlockSpec((ne, d), lambda i: (0, 0)),
            pl.BlockSpec((ne, 1), lambda i: (0, 0)),
            pl.BlockSpec((tt, tt), lambda i: (0, 0)),
            pl.BlockSpec((ne, ne), lambda i: (0, 0)),
        ],
        out_specs=[slot_spec, slot_spec, slot_spec, pl.BlockSpec((ne, 128), lambda i: (0, 0))],
        out_shape=[jax.ShapeDtypeStruct((TOP_K, t), jnp.int32), jax.ShapeDtypeStruct((TOP_K, t), jnp.int32),
                   jax.ShapeDtypeStruct((TOP_K, t), jnp.float32), jax.ShapeDtypeStruct((ne, 128), jnp.float32)],
        scratch_shapes=[pltpu.VMEM((ne, 1), jnp.float32)],
        compiler_params=_cp(("arbitrary",)), name="moe_router",
    )(hn, _bf16(router_w.T), router_bias.reshape(ne, 1).astype(jnp.float32), ut, lt)


def _dispatch_kernel(last_ref, dest_ref, src_ref, out_ref, zbuf, sem, zsem, *, tt):
    bm = zbuf.shape[0]

    @pl.when(pl.program_id(0) == 0)
    def _():
        zbuf[...] = jnp.zeros_like(zbuf)

        def zero_copy(e):
            return pltpu.make_async_copy(zbuf, out_ref.at[pl.ds(pl.multiple_of(last_ref[e], bm), bm)], zsem)

        def zstart(e, carry):
            @pl.when(last_ref[e] >= 0)
            def _():
                zero_copy(e).start()
            return carry

        def zwait(e, carry):
            @pl.when(last_ref[e] >= 0)
            def _():
                zero_copy(e).wait()
            return carry

        lax.fori_loop(0, last_ref.shape[0], zstart, 0)
        lax.fori_loop(0, last_ref.shape[0], zwait, 0)

    def body(t, carry):
        for k in range(TOP_K):
            pltpu.make_async_copy(src_ref.at[pl.ds(t, 1)], out_ref.at[pl.ds(dest_ref[k, t], 1)], sem).start()
        return carry

    lax.fori_loop(0, tt, body, 0)
    for k in range(TOP_K):
        pltpu.make_async_copy(src_ref, out_ref.at[pl.ds(0, tt)], sem).wait()


def moe_dispatch(zero_blocks, dest, hp, n_rows, bm, tt=256):
    t, w = hp.shape
    tt = min(tt, t)
    grid_spec = pltpu.PrefetchScalarGridSpec(
        num_scalar_prefetch=1, grid=(t // tt,),
        in_specs=[
            pl.BlockSpec((TOP_K, tt), lambda i, last: (0, i), memory_space=pltpu.SMEM),
            pl.BlockSpec((tt, w), lambda i, last: (i, 0)),
        ],
        out_specs=pl.BlockSpec(memory_space=pl.ANY),
        scratch_shapes=[pltpu.VMEM((bm, w), jnp.uint32), pltpu.SemaphoreType.DMA(()), pltpu.SemaphoreType.DMA(())],
    )
    return pl.pallas_call(
        functools.partial(_dispatch_kernel, tt=tt), grid_spec=grid_spec,
        out_shape=jax.ShapeDtypeStruct((n_rows, w), jnp.uint32),
        compiler_params=pltpu.CompilerParams(dimension_semantics=("arbitrary",), has_side_effects=True),
        name="moe_dispatch",
    )(zero_blocks, dest, hp)


def _expert_kernel(be_ref, nu_ref, x_ref, wg_ref, wu_ref, wd_ref, o_ref, wg_s, wu_s, wd_s):
    b = pl.program_id(0)
    e = be_ref[b]
    prev = be_ref[jnp.maximum(b - 1, 0)]
    half = x_ref.shape[1]

    @pl.when((b == 0) | (e != prev))
    def _():
        wg_s[0] = _bf16(wg_ref[0:half, :])
        wg_s[1] = _bf16(wg_ref[half:, :])
        wu_s[0] = _bf16(wu_ref[0:half, :])
        wu_s[1] = _bf16(wu_ref[half:, :])
        wd_s[...] = _bf16(wd_ref[...])

    @pl.when(b < nu_ref[0])
    def _():
        xp = x_ref[...]
        xa = _bf16(_unpack_hi(xp))
        xb = _bf16(_unpack_lo(xp))
        g = _dot(xa, wg_s[0]) + _dot(xb, wg_s[1])
        u = _dot(xa, wu_s[0]) + _dot(xb, wu_s[1])
        h = _bf16(g * _sigmoid(g) * u)
        y = _dot(h, wd_s[...])
        o_ref[...] = _pack_bf16_pair(y[:, :half], y[:, half:])

    @pl.when(b >= nu_ref[0])
    def _():
        o_ref[...] = jnp.zeros_like(o_ref)


def moe_experts(block_expert, n_used, xs, w_gate, w_up, w_down, layer, bm):
    n_rows, half = xs.shape
    _, ne, d, f = w_gate.shape
    nblk = n_rows // bm
    grid_spec = pltpu.PrefetchScalarGridSpec(
        num_scalar_prefetch=2, grid=(nblk,),
        in_specs=[
            pl.BlockSpec((bm, half), lambda b, be, nu: (jnp.minimum(b, nu[0] - 1), 0)),
            pl.BlockSpec((None, None, d, f), lambda b, be, nu: (layer, be[b], 0, 0)),
            pl.BlockSpec((None, None, d, f), lambda b, be, nu: (layer, be[b], 0, 0)),
            pl.BlockSpec((None, None, f, d), lambda b, be, nu: (layer, be[b], 0, 0)),
        ],
        out_specs=pl.BlockSpec((bm, half), lambda b, be, nu: (b, 0)),
        scratch_shapes=[pltpu.VMEM((2, half, f), jnp.bfloat16), pltpu.VMEM((2, half, f), jnp.bfloat16),
                        pltpu.VMEM((f, d), jnp.bfloat16)],
    )
    return pl.pallas_call(
        _expert_kernel, grid_spec=grid_spec,
        out_shape=jax.ShapeDtypeStruct((n_rows, half), jnp.uint32),
        compiler_params=_cp(("arbitrary",)), name="moe_experts",
    )(block_expert, n_used, xs, w_gate, w_up, w_down)


def _combine_kernel(dest_ref, nxt_ref, ys_ref, gk_ref, sh_ref, x_ref, ga_ref, o_ref, buf, sems, *, tt):
    i = pl.program_id(0)
    n = pl.num_programs(0)
    slot = i % 2

    def gather(d_ref, s):
        def body(t, carry):
            for k in range(TOP_K):
                pltpu.make_async_copy(ys_ref.at[pl.ds(d_ref[k, t], 1)], buf.at[s, k, pl.ds(t, 1)], sems.at[s]).start()
            return carry

        lax.fori_loop(0, tt, body, 0)

    @pl.when(i == 0)
    def _():
        gather(dest_ref, 0)

    @pl.when(i + 1 < n)
    def _():
        gather(nxt_ref, 1 - slot)

    for k in range(TOP_K):
        pltpu.make_async_copy(ys_ref.at[pl.ds(0, tt)], buf.at[slot, k], sems.at[slot]).wait()
    half = buf.shape[3]
    acc_a = jnp.zeros((tt, half), jnp.float32)
    acc_b = jnp.zeros((tt, half), jnp.float32)
    for k in range(TOP_K):
        p = buf[slot, k]
        gk = gk_ref[:, k:k + 1]
        acc_a = acc_a + gk * _unpack_hi(p)
        acc_b = acc_b + gk * _unpack_lo(p)
    sh = sh_ref[...].astype(jnp.float32)
    o_ref[:, 0:half] = x_ref[:, 0:half] + ga_ref[:, 0:half] * (acc_a + sh[:, 0:half])
    o_ref[:, half:] = x_ref[:, half:] + ga_ref[:, half:] * (acc_b + sh[:, half:])


def moe_combine(dest, ys, gates_tk, shared, x2d, gate_ada, seq, tt=128):
    t, d = x2d.shape
    half = ys.shape[1]
    tt = min(tt, seq)
    nt = t // tt
    return pl.pallas_call(
        functools.partial(_combine_kernel, tt=tt), grid=(nt,),
        in_specs=[
            pl.BlockSpec((TOP_K, tt), lambda i: (0, i), memory_space=pltpu.SMEM),
            pl.BlockSpec((TOP_K, tt), lambda i: (0, jnp.minimum(i + 1, nt - 1)), memory_space=pltpu.SMEM),
            pl.BlockSpec(memory_space=pl.ANY),
            pl.BlockSpec((tt, TOP_K), lambda i: (i, 0)),
            pl.BlockSpec((tt, d), lambda i: (i, 0)),
            pl.BlockSpec((tt, d), lambda i: (i, 0)),
            pl.BlockSpec((None, 1, d), lambda i: ((i * tt) // seq, 0, 0)),
        ],
        out_specs=pl.BlockSpec((tt, d), lambda i: (i, 0)),
        out_shape=jax.ShapeDtypeStruct((t, d), jnp.float32),
        scratch_shapes=[pltpu.VMEM((2, TOP_K, tt, half), jnp.uint32), pltpu.SemaphoreType.DMA((2,))],
        compiler_params=_cp(("arbitrary",)), name="moe_combine",
    )(dest, dest, ys, gates_tk, shared, x2d, gate_ada)


MOE_ROWS = 256


def moe_layer(x2d, hn, hp, gate_ada, seq, router_w, router_bias, w_gate, w_up, w_down, layer, s_gate, s_up, s_down):
    t, d = x2d.shape
    ne = router_w.shape[1]
    bm = min(MOE_ROWS, t)
    eid, posk, gates_k, cnt = moe_router(hn, router_w, router_bias)
    counts = cnt[:, 0].astype(jnp.int32)
    padded = (counts + bm - 1) // bm * bm
    pad_end = jnp.cumsum(padded)
    pad_start = pad_end - padded
    n_rows = t * TOP_K + ne * bm
    nblk = n_rows // bm
    dest = jnp.take(pad_start, eid) + posk
    block_row = jnp.arange(nblk, dtype=jnp.int32) * bm
    block_expert = jnp.minimum(jnp.sum(pad_end[None, :] <= block_row[:, None], axis=1), ne - 1).astype(jnp.int32)
    n_used = (pad_end[-1:] // bm).astype(jnp.int32)
    tail = (n_used[0] + jnp.arange(ne, dtype=jnp.int32)) * bm
    zero_blocks = jnp.concatenate([jnp.where(padded > counts, pad_end - bm, -1),
                                   jnp.where(tail < n_rows, tail, -1)]).astype(jnp.int32)
    xs = moe_dispatch(zero_blocks, dest, hp, n_rows, bm)
    ys = moe_experts(block_expert, n_used, xs, w_gate, w_up, w_down, layer, bm)
    hs = matmul_swiglu(hn, _bf16(s_gate), _bf16(s_up))
    shared = matmul(hs, _bf16(s_down), jnp.bfloat16)
    return moe_combine(dest, ys, gates_k.T, shared, x2d, gate_ada, seq)


def kernel(x, c, mix_norm_g, mix_ada_w, mix_ada_b, ev_w_in, ev_conv_w, ev_conv_b, ev_ln_g, ev_ln_b, ev_q_norm_g, ev_k_norm_g, ev_w_out, od_w_in, od_log_dt, od_lambda_re, od_lambda_im, od_b_re, od_b_im, od_c_re, od_c_im, od_d, od_w_out, ffn_norm_g, ffn_ada_w, ffn_ada_b, router_w, router_bias, exp_gate, exp_up, exp_down, sh_gate, sh_up, sh_down):
    bsz, seq, d = x.shape
    depth = mix_norm_g.shape[0]
    t = bsz * seq
    c_pad = jnp.zeros((8, d), jnp.float32).at[:bsz].set(c)
    mix_mod = adaln_all(c_pad, mix_ada_w, mix_ada_b)[:, :bsz]
    ffn_mod = adaln_all(c_pad, ffn_ada_w, ffn_ada_b)[:, :bsz]

    def split(mod):
        return tuple(mod[:, None, k * d:(k + 1) * d] for k in range(3))

    for i in range(depth):
        j = i // 2
        shift, scale, gate = split(mix_mod[i])
        hn = norm_modulate(x, mix_norm_g[i], scale, shift)
        x2d = x.reshape(t, d)
        if i % 2 == 0:
            cc = ev_conv_w.shape[-1]
            dh = ev_q_norm_g.shape[-1]
            sbw = (ev_w_in.shape[-1] - 2 * cc) // 3
            proj = matmul(hn, _bf16(ev_w_in[j]), jnp.bfloat16)
            a = conv_branch(proj, ev_conv_w[j], ev_conv_b[j], ev_ln_g[j], ev_ln_b[j], seq)
            qn, kn = qk_norm(proj, ev_q_norm_g[j], ev_k_norm_g[j], 2 * cc, sbw)
            o = stickbreak_attention(qn, kn, proj, 2 * cc + 2 * sbw, bsz, seq, dh)
            x2d = matmul2_resid(a, o, _bf16(ev_w_out[j]), x2d, gate, seq)
        else:
            u = matmul(hn, _bf16(od_w_in[j]), jnp.float32)
            wts = s5_weights(od_log_dt[j], od_lambda_re[j], od_lambda_im[j], od_b_re[j], od_b_im[j],
                             od_c_re[j], od_c_im[j], od_d[j])
            y = s5_scan(u.reshape(bsz, seq, -1), *wts)
            x2d = matmul_glu_resid(y.reshape(t, -1), _bf16(od_w_out[j]), x2d, gate, seq)
        x = x2d.reshape(bsz, seq, d)
        shift, scale, gate = split(ffn_mod[i])
        hn, hp = norm_modulate(x, ffn_norm_g[i], scale, shift, packed=True)
        x2d = moe_layer(x.reshape(t, d), hn, hp, gate, seq, router_w[i], router_bias[i], exp_gate, exp_up,
                        exp_down, i, sh_gate[i], sh_up[i], sh_down[i])
        x = x2d.reshape(bsz, seq, d)
    return x
```

```python
import functools
import math

import jax
import jax.numpy as jnp
from jax import lax
from jax.experimental import pallas as pl
from jax.experimental.pallas import tpu as pltpu

EPS = 1e-6
TOP_K = 8
N_GROUPS = 8
TOPK_GROUPS = 4
ROUTED_SCALE = 2.5
S5_CHUNK = 8
S5_BLOCK_LANES = 128

V7X_VMEM_LIMIT = 56 * 1024 * 1024


def _cp(dims, vmem=V7X_VMEM_LIMIT):
    return pltpu.CompilerParams(dimension_semantics=dims, vmem_limit_bytes=vmem)


def _sigmoid(x):
    return 1.0 / (1.0 + jnp.exp(-x))


def _bf16(x):
    return x.astype(jnp.bfloat16)


def _dot(a, b):
    return jnp.dot(a, b, preferred_element_type=jnp.float32)


def _tile(n, target, quantum=128):
    if n <= target:
        return n
    best = quantum
    for cand in range(quantum, target + 1, quantum):
        if n % cand == 0:
            best = cand
    assert n % best == 0, (n, target)
    return best


def _dot_nt(a, b):
    return lax.dot_general(a, b, (((1,), (1,)), ((), ())), preferred_element_type=jnp.float32)


def _adaln_kernel(c_ref, w_ref, b_ref, o_ref):
    c = c_ref[...]
    sc = _bf16(c * _sigmoid(c))
    o_ref[...] = _dot(sc, _bf16(w_ref[...])) + b_ref[...]


def adaln_all(c_pad, w, b, tn=512):
    nl, d, n = w.shape
    tn = min(tn, n)
    return pl.pallas_call(
        _adaln_kernel,
        grid=(nl, n // tn),
        in_specs=[
            pl.BlockSpec((8, d), lambda l, j: (0, 0)),
            pl.BlockSpec((None, d, tn), lambda l, j: (l, 0, j)),
            pl.BlockSpec((None, 1, tn), lambda l, j: (l, 0, j)),
        ],
        out_specs=pl.BlockSpec((None, 8, tn), lambda l, j: (l, 0, j)),
        out_shape=jax.ShapeDtypeStruct((nl, 8, n), jnp.float32),
        compiler_params=_cp(("parallel", "parallel")),
        name="adaln",
    )(c_pad, w, b.reshape(nl, 1, n))


def _pack_bf16_pair(a, b):
    ua = pltpu.bitcast(_bf16(a).astype(jnp.float32), jnp.uint32)
    ub = pltpu.bitcast(_bf16(b).astype(jnp.float32), jnp.uint32)
    return (ua & jnp.uint32(0xFFFF0000)) | (ub >> 16)


def _unpack_hi(p):
    return pltpu.bitcast(p & jnp.uint32(0xFFFF0000), jnp.float32)


def _unpack_lo(p):
    return pltpu.bitcast(p << 16, jnp.float32)


def _norm_mod(x, g_ref, sc_ref, sh_ref):
    xf = x.astype(jnp.float32)
    ms = jnp.mean(xf * xf, axis=-1, keepdims=True)
    return xf * lax.rsqrt(ms + EPS) * g_ref[...] * (1.0 + sc_ref[...]) + sh_ref[...]


def _norm_kernel(x_ref, g_ref, sc_ref, sh_ref, o_ref):
    o_ref[...] = _bf16(_norm_mod(x_ref[...], g_ref, sc_ref, sh_ref))


def _norm_pack_kernel(x_ref, g_ref, sc_ref, sh_ref, o_ref, p_ref):
    y = _norm_mod(x_ref[...], g_ref, sc_ref, sh_ref)
    o_ref[...] = _bf16(y)
    h = y.shape[-1] // 2
    p_ref[...] = _pack_bf16_pair(y[:, :h], y[:, h:])


def norm_modulate(x, g, scale, shift, packed=False, ts=256):
    b, s, d = x.shape
    ts = min(ts, s)
    nt = s // ts
    in_specs = [
        pl.BlockSpec((None, ts, d), lambda i, j: (i, j, 0)),
        pl.BlockSpec((1, d), lambda i, j: (0, 0)),
        pl.BlockSpec((None, 1, d), lambda i, j: (i, 0, 0)),
        pl.BlockSpec((None, 1, d), lambda i, j: (i, 0, 0)),
    ]
    o_spec = pl.BlockSpec((ts, d), lambda i, j: (i * nt + j, 0))
    o_shape = jax.ShapeDtypeStruct((b * s, d), jnp.bfloat16)
    if not packed:
        return pl.pallas_call(
            _norm_kernel, grid=(b, nt), in_specs=in_specs, out_specs=o_spec, out_shape=o_shape,
            compiler_params=_cp(("parallel", "parallel")), name="norm_mod",
        )(x, g.reshape(1, d), scale, shift)
    return pl.pallas_call(
        _norm_pack_kernel, grid=(b, nt), in_specs=in_specs,
        out_specs=[o_spec, pl.BlockSpec((ts, d // 2), lambda i, j: (i * nt + j, 0))],
        out_shape=[o_shape, jax.ShapeDtypeStruct((b * s, d // 2), jnp.uint32)],
        compiler_params=_cp(("parallel", "parallel")), name="norm_mod_pack",
    )(x, g.reshape(1, d), scale, shift)


def _mm_kernel(a_ref, w_ref, o_ref):
    o_ref[...] = _dot(_bf16(a_ref[...]), w_ref[...]).astype(o_ref.dtype)


def matmul(a, w, out_dtype, bm=1024, bn=1024):
    m, k = a.shape
    n = w.shape[1]
    bm, bn = _tile(m, bm), _tile(n, bn)
    return pl.pallas_call(
        _mm_kernel, grid=(m // bm, n // bn),
        in_specs=[pl.BlockSpec((bm, k), lambda i, j: (i, 0)), pl.BlockSpec((k, bn), lambda i, j: (0, j))],
        out_specs=pl.BlockSpec((bm, bn), lambda i, j: (i, j)),
        out_shape=jax.ShapeDtypeStruct((m, n), out_dtype),
        compiler_params=_cp(("parallel", "parallel")), name="matmul",
    )(a, w)


def _mm2_resid_kernel(a1_ref, a2_ref, w1_ref, w2_ref, x_ref, g_ref, o_ref):
    m = _dot(a1_ref[...], w1_ref[...]) + _dot(a2_ref[...], w2_ref[...])
    o_ref[...] = x_ref[...] + g_ref[...] * m


def matmul2_resid(a1, a2, w, x2d, gate, seq, bm=1024, bn=1024):
    m, k1 = a1.shape
    k2 = a2.shape[1]
    assert k1 == k2
    n = w.shape[1]
    bm, bn = _tile(seq, bm), _tile(n, bn)
    return pl.pallas_call(
        _mm2_resid_kernel, grid=(m // bm, n // bn),
        in_specs=[
            pl.BlockSpec((bm, k1), lambda i, j: (i, 0)),
            pl.BlockSpec((bm, k2), lambda i, j: (i, 0)),
            pl.BlockSpec((k1, bn), lambda i, j: (0, j)),
            pl.BlockSpec((k2, bn), lambda i, j: (1, j)),
            pl.BlockSpec((bm, bn), lambda i, j: (i, j)),
            pl.BlockSpec((None, 1, bn), lambda i, j: ((i * bm) // seq, 0, j)),
        ],
        out_specs=pl.BlockSpec((bm, bn), lambda i, j: (i, j)),
        out_shape=jax.ShapeDtypeStruct((m, n), jnp.float32),
        compiler_params=_cp(("parallel", "parallel")), name="matmul2_resid",
    )(a1, a2, w, w, x2d, gate)


def _mm_glu_resid_kernel(a_ref, wa_ref, wb_ref, x_ref, g_ref, o_ref):
    a = _bf16(a_ref[...])
    va = _dot(a, wa_ref[...])
    vb = _dot(a, wb_ref[...])
    o_ref[...] = x_ref[...] + g_ref[...] * (va * _sigmoid(vb))


def matmul_glu_resid(a, w, x2d, gate, seq, bm=512, bn=512):
    m, k = a.shape
    n = w.shape[1] // 2
    bm, bn = _tile(seq, bm), _tile(n, bn)
    nb = n // bn
    return pl.pallas_call(
        _mm_glu_resid_kernel, grid=(m // bm, nb),
        in_specs=[
            pl.BlockSpec((bm, k), lambda i, j: (i, 0)),
            pl.BlockSpec((k, bn), lambda i, j: (0, j)),
            pl.BlockSpec((k, bn), lambda i, j: (0, j + nb)),
            pl.BlockSpec((bm, bn), lambda i, j: (i, j)),
            pl.BlockSpec((None, 1, bn), lambda i, j: ((i * bm) // seq, 0, j)),
        ],
        out_specs=pl.BlockSpec((bm, bn), lambda i, j: (i, j)),
        out_shape=jax.ShapeDtypeStruct((m, n), jnp.float32),
        compiler_params=_cp(("parallel", "parallel")), name="matmul_glu_resid",
    )(a, w, w, x2d, gate)


def _mm_swiglu_kernel(a_ref, wg_ref, wu_ref, o_ref):
    a = a_ref[...]
    g = _dot(a, wg_ref[...])
    u = _dot(a, wu_ref[...])
    o_ref[...] = _bf16(g * _sigmoid(g) * u)


def matmul_swiglu(a, wg, wu, bm=1024, bn=512):
    m, k = a.shape
    n = wg.shape[1]
    bm, bn = _tile(m, bm), _tile(n, bn)
    return pl.pallas_call(
        _mm_swiglu_kernel, grid=(m // bm, n // bn),
        in_specs=[
            pl.BlockSpec((bm, k), lambda i, j: (i, 0)),
            pl.BlockSpec((k, bn), lambda i, j: (0, j)),
            pl.BlockSpec((k, bn), lambda i, j: (0, j)),
        ],
        out_specs=pl.BlockSpec((bm, bn), lambda i, j: (i, j)),
        out_shape=jax.ShapeDtypeStruct((m, n), jnp.bfloat16),
        compiler_params=_cp(("parallel", "parallel")), name="matmul_swiglu",
    )(a, wg, wu)


CONV_HALO = 32
CONV_ROWS = 64
CONV_LANES = 256


def _conv_kernel(v_ref, g_ref, vh_ref, gh_ref, w_ref, cb_ref, lg_ref, lb_ref, o_ref, abuf, cbuf, *,
                 tiles_per_seq, width):
    ts, cc = o_ref.shape
    first = (pl.program_id(0) % tiles_per_seq) == 0
    vh = vh_ref[...].astype(jnp.float32)
    gh = gh_ref[...].astype(jnp.float32)
    abuf[0:CONV_HALO, :] = jnp.where(first, 0.0, vh * _sigmoid(gh))
    v = v_ref[...].astype(jnp.float32)
    g = g_ref[...].astype(jnp.float32)
    abuf[CONV_HALO:, :] = v * _sigmoid(g)
    off = CONV_HALO - (width - 1)
    for lc in range(cc // CONV_LANES):
        ls = slice(lc * CONV_LANES, (lc + 1) * CONV_LANES)
        for rc in range(ts // CONV_ROWS):
            r0 = rc * CONV_ROWS
            acc = jnp.zeros((CONV_ROWS, CONV_LANES), jnp.float32)
            for j in range(width):
                acc = acc + w_ref[j:j + 1, ls] * abuf[r0 + off + j:r0 + off + j + CONV_ROWS, ls]
            cbuf[r0:r0 + CONV_ROWS, ls] = acc + cb_ref[:, ls]
    c = cbuf[...]
    mu = jnp.mean(c, axis=-1, keepdims=True)
    cen = c - mu
    var = jnp.mean(cen * cen, axis=-1, keepdims=True)
    y = cen * lax.rsqrt(var + EPS) * lg_ref[...] + lb_ref[...]
    o_ref[...] = _bf16(y * _sigmoid(y))


def conv_branch(proj, conv_w, conv_b, ln_g, ln_b, seq, ts=128):
    t = proj.shape[0]
    width, cc = conv_w.shape
    ts = min(ts, seq)
    assert width - 1 <= CONV_HALO and ts % CONV_HALO == 0 and cc % CONV_LANES == 0 and ts % CONV_ROWS == 0
    r = ts // CONV_HALO
    halo_idx = lambda i: jnp.maximum(i * r - 1, 0)
    kern = functools.partial(_conv_kernel, tiles_per_seq=seq // ts, width=width)
    return pl.pallas_call(
        kern, grid=(t // ts,),
        in_specs=[
            pl.BlockSpec((ts, cc), lambda i: (i, 0)),
            pl.BlockSpec((ts, cc), lambda i: (i, 1)),
            pl.BlockSpec((CONV_HALO, cc), lambda i: (halo_idx(i), 0)),
            pl.BlockSpec((CONV_HALO, cc), lambda i: (halo_idx(i), 1)),
            pl.BlockSpec((width, cc), lambda i: (0, 0)),
            pl.BlockSpec((1, cc), lambda i: (0, 0)),
            pl.BlockSpec((1, cc), lambda i: (0, 0)),
            pl.BlockSpec((1, cc), lambda i: (0, 0)),
        ],
        out_specs=pl.BlockSpec((ts, cc), lambda i: (i, 0)),
        out_shape=jax.ShapeDtypeStruct((t, cc), jnp.bfloat16),
        scratch_shapes=[pltpu.VMEM((CONV_HALO + ts, cc), jnp.float32), pltpu.VMEM((ts, cc), jnp.float32)],
        compiler_params=_cp(("parallel",)), name="conv_branch",
    )(proj, proj, proj, proj, conv_w, conv_b.reshape(1, cc), ln_g.reshape(1, cc), ln_b.reshape(1, cc))


def _qknorm_kernel(q_ref, k_ref, gq_ref, gk_ref, qo_ref, ko_ref, *, dh):
    q_scale = math.log2(math.e) / math.sqrt(dh)
    for src, g_ref, dst, mul in ((q_ref, gq_ref, qo_ref, q_scale), (k_ref, gk_ref, ko_ref, 1.0)):
        for h in range(src.shape[1] // dh):
            ls = slice(h * dh, (h + 1) * dh)
            xf = src[:, ls].astype(jnp.float32)
            ms = jnp.mean(xf * xf, axis=-1, keepdims=True)
            dst[:, ls] = _bf16(xf * lax.rsqrt(ms + EPS) * g_ref[...] * mul)


def qk_norm(proj, gq, gk, col0, width, ts=512):
    t = proj.shape[0]
    dh = gq.shape[0]
    ts = min(ts, t)
    qb, kb = col0 // width, col0 // width + 1
    return pl.pallas_call(
        functools.partial(_qknorm_kernel, dh=dh), grid=(t // ts,),
        in_specs=[
            pl.BlockSpec((ts, width), lambda i: (i, qb)),
            pl.BlockSpec((ts, width), lambda i: (i, kb)),
            pl.BlockSpec((1, dh), lambda i: (0, 0)),
            pl.BlockSpec((1, dh), lambda i: (0, 0)),
        ],
        out_specs=[pl.BlockSpec((ts, width), lambda i: (i, 0)), pl.BlockSpec((ts, width), lambda i: (i, 0))],
        out_shape=[jax.ShapeDtypeStruct((t, width), jnp.bfloat16)] * 2,
        compiler_params=_cp(("parallel",)), name="qk_norm",
    )(proj, proj, gq.reshape(1, dh), gk.reshape(1, dh))


ATT_TILE = 1024
ATT_SUB = 256


def _attn_kernel(qi_ref, kj_ref, q_ref, k_ref, v_ref, u_ref, m_ref, o_ref, acc_ref, r_ref):
    p = pl.program_id(2)
    qi = qi_ref[p]
    kj = kj_ref[p]
    tq = q_ref.shape[0]
    tk = k_ref.shape[0]
    sub = u_ref.shape[1]

    @pl.when(kj == qi)
    def _():
        acc_ref[...] = jnp.zeros_like(acc_ref)
        r_ref[...] = jnp.zeros_like(r_ref)

    def process(diagonal):
        nsb = tk // sub
        z = _dot_nt(q_ref[...], k_ref[...])
        zb = _bf16(z)
        one = jnp.ones((), jnp.bfloat16)
        log2e = jnp.asarray(math.log2(math.e), jnp.bfloat16)
        lk = -(jnp.maximum(zb, 0) + jnp.log(one + jnp.exp2(-jnp.abs(zb))) * log2e)
        lm = lk * m_ref[...] if diagonal else lk
        r = r_ref[...]
        acc = acc_ref[...]
        for s in reversed(range(nsb)):
            ks = slice(s * sub, (s + 1) * sub)
            cs = _dot(lm[:, ks], u_ref[...])
            a = _bf16(jnp.exp2(z[:, ks] + lk[:, ks].astype(jnp.float32) + (cs + r)))
            if diagonal:
                a = a * m_ref[:, ks]
            acc = acc + _dot(a, v_ref[ks, :])
            r = r + cs[:, 0:1] + lm[:, s * sub:s * sub + 1].astype(jnp.float32)
        r_ref[...] = r
        acc_ref[...] = acc

    @pl.when(kj == qi)
    def _():
        process(True)

    @pl.when(kj != qi)
    def _():
        process(False)

    @pl.when(kj == 0)
    def _():
        o_ref[...] = acc_ref[...].astype(o_ref.dtype)


def stickbreak_attention(qn, kn, proj, vcol0, batch, seq, dh):
    t, w = qn.shape
    heads = w // dh
    tile = min(ATT_TILE, seq)
    sub = min(ATT_SUB, tile)
    nq = seq // tile
    pairs = [(i, j) for i in range(nq) for j in range(i, -1, -1)]
    qi = jnp.asarray([p[0] for p in pairs], jnp.int32)
    kj = jnp.asarray([p[1] for p in pairs], jnp.int32)
    r = lax.broadcasted_iota(jnp.int32, (sub, sub), 0)
    c = lax.broadcasted_iota(jnp.int32, (sub, sub), 1)
    tri = _bf16((r > c).astype(jnp.float32))
    rt = lax.broadcasted_iota(jnp.int32, (tile, tile), 0)
    ct = lax.broadcasted_iota(jnp.int32, (tile, tile), 1)
    causal = _bf16((ct < rt).astype(jnp.float32))
    vb = vcol0 // dh
    grid_spec = pltpu.PrefetchScalarGridSpec(
        num_scalar_prefetch=2,
        grid=(batch, heads, len(pairs)),
        in_specs=[
            pl.BlockSpec((tile, dh), lambda b, h, p, qi, kj: (b * nq + qi[p], h)),
            pl.BlockSpec((tile, dh), lambda b, h, p, qi, kj: (b * nq + kj[p], h)),
            pl.BlockSpec((tile, dh), lambda b, h, p, qi, kj: (b * nq + kj[p], vb + h)),
            pl.BlockSpec((sub, sub), lambda b, h, p, qi, kj: (0, 0)),
            pl.BlockSpec((tile, tile), lambda b, h, p, qi, kj: (0, 0)),
        ],
        out_specs=pl.BlockSpec((tile, dh), lambda b, h, p, qi, kj: (b * nq + qi[p], h)),
        scratch_shapes=[pltpu.VMEM((tile, dh), jnp.float32), pltpu.VMEM((tile, 1), jnp.float32)],
    )
    return pl.pallas_call(
        _attn_kernel, grid_spec=grid_spec,
        out_shape=jax.ShapeDtypeStruct((t, w), jnp.bfloat16),
        compiler_params=_cp(("parallel", "parallel", "arbitrary")), name="stickbreak_attn",
    )(qi, kj, qn, kn, proj, tri, causal)


def _gelu_tanh(x):
    return 0.5 * x * (1.0 + jnp.tanh(math.sqrt(2.0 / math.pi) * (x + 0.044715 * (x * x * x))))


def _s5_kernel(u_ref, wm_ref, we_ref, wc_ref, lam_ref, y_ref, s_ref, e_ref, sp_ref):
    tt, bl = u_ref.shape
    nch = tt // S5_CHUNK
    half = s_ref.shape[1]

    @pl.when(pl.program_id(2) == 0)
    def _():
        s_ref[...] = jnp.zeros_like(s_ref)

    ucat = _bf16(jnp.concatenate(
        [u_ref[pl.ds(j, nch, stride=S5_CHUNK), :] for j in range(S5_CHUNK)], axis=1))
    e_ref[...] = _dot(ucat, we_ref[...])
    lam_re = lam_ref[0:1, :]
    lam_im = lam_ref[1:2, :]

    def body(c, carry):
        sre, sim = carry
        sp_ref[pl.ds(c, 1), 0:half] = sre
        sp_ref[pl.ds(c, 1), half:] = sim
        ere = e_ref[pl.ds(c, 1), 0:half]
        eim = e_ref[pl.ds(c, 1), half:]
        return (lam_re * sre - lam_im * sim + ere, lam_re * sim + lam_im * sre + eim)

    sre, sim = lax.fori_loop(0, nch, body, (s_ref[0:1, :], s_ref[1:2, :]))
    s_ref[0:1, :] = sre
    s_ref[1:2, :] = sim
    ycat = _dot(ucat, wm_ref[...]) + _dot(_bf16(sp_ref[...]), wc_ref[...])
    for j in range(S5_CHUNK):
        y_ref[pl.ds(j, nch, stride=S5_CHUNK), :] = _gelu_tanh(ycat[:, j * bl:(j + 1) * bl])


def s5_weights(log_dt, lam_re, lam_im, b_re, b_im, c_re, c_im, d_skip):
    g, p = lam_re.shape
    hg = b_re.shape[-1]
    L = S5_CHUNK
    gb = S5_BLOCK_LANES // hg
    nb = g // gb
    dt = jnp.exp(log_dt)[:, None]
    mag = jnp.exp(lam_re * dt)
    ang = lam_im * dt
    lbar_re = mag * jnp.cos(ang)
    lbar_im = mag * jnp.sin(ang)
    den = lam_re * lam_re + lam_im * lam_im
    nr = lbar_re - 1.0
    f_re = (nr * lam_re + lbar_im * lam_im) / den
    f_im = (lbar_im * lam_re - nr * lam_im) / den
    bbar_re = f_re[..., None] * b_re - f_im[..., None] * b_im
    bbar_im = f_re[..., None] * b_im + f_im[..., None] * b_re
    n = jnp.arange(L + 1, dtype=jnp.float32)[:, None, None]
    pw_mag = jnp.exp(n * (lam_re * dt)[None])
    pw_re = pw_mag * jnp.cos(n * ang[None])
    pw_im = pw_mag * jnp.sin(n * ang[None])
    lb_re = pw_re[..., None] * bbar_re[None] - pw_im[..., None] * bbar_im[None]
    lb_im = pw_re[..., None] * bbar_im[None] + pw_im[..., None] * bbar_re[None]
    ktau = (jnp.einsum('gop,ngpi->ngoi', c_re, lb_re[:L], precision='highest')
            - jnp.einsum('gop,ngpi->ngoi', c_im, lb_im[:L], precision='highest'))
    jj = jnp.arange(L)[:, None]
    tt = jnp.arange(L)[None, :]
    tau = jnp.clip(tt - jj, 0, L - 1)
    kjt = jnp.where((tt >= jj)[..., None, None, None], ktau[tau], 0.0)
    kjt = kjt + (jj == tt)[..., None, None, None] * (d_skip[:, :, None] * jnp.eye(hg))[None, None]
    eye = jnp.eye(gb, dtype=jnp.float32)

    def block_diag(small):
        full = _bf16(small[:, :, :, :, :, None, :] * eye[None, None, :, None, None, :, None])
        return full.reshape(full.shape[0], full.shape[1] * full.shape[2] * full.shape[3], -1)

    w_main = block_diag(kjt.reshape(L, L, nb, gb, hg, hg).transpose(2, 0, 3, 5, 1, 4))
    le_re = lb_re[L - 1 - jnp.arange(L)].reshape(L, nb, gb, p, hg)
    le_im = lb_im[L - 1 - jnp.arange(L)].reshape(L, nb, gb, p, hg)
    le = jnp.stack([le_re, le_im], axis=0)
    w_end = block_diag(le.transpose(2, 1, 3, 5, 0, 4))
    m_re = c_re[None] * pw_re[1:, :, None, :] - c_im[None] * pw_im[1:, :, None, :]
    m_im = c_re[None] * pw_im[1:, :, None, :] + c_im[None] * pw_re[1:, :, None, :]
    mc = jnp.stack([m_re, -m_im], axis=0).reshape(2, L, nb, gb, hg, p)
    w_carry = block_diag(mc.transpose(2, 0, 3, 5, 1, 4))
    lam_l = jnp.stack([pw_re[L].reshape(nb, gb * p), pw_im[L].reshape(nb, gb * p)], axis=1)
    return w_main, w_end, w_carry, lam_l


def s5_scan(u, w_main, w_end, w_carry, lam_l, tt=2048):
    b, s, w = u.shape
    nb, kdim, sdim = w_end.shape
    bl = S5_BLOCK_LANES
    tt = min(tt, s)
    nch = tt // S5_CHUNK
    return pl.pallas_call(
        _s5_kernel, grid=(b, nb, s // tt),
        in_specs=[
            pl.BlockSpec((None, tt, bl), lambda i, j, t: (i, t, j)),
            pl.BlockSpec((None, kdim, kdim), lambda i, j, t: (j, 0, 0)),
            pl.BlockSpec((None, kdim, sdim), lambda i, j, t: (j, 0, 0)),
            pl.BlockSpec((None, sdim, kdim), lambda i, j, t: (j, 0, 0)),
            pl.BlockSpec((None, 2, sdim // 2), lambda i, j, t: (j, 0, 0)),
        ],
        out_specs=pl.BlockSpec((None, tt, bl), lambda i, j, t: (i, t, j)),
        out_shape=jax.ShapeDtypeStruct((b, s, w), jnp.float32),
        scratch_shapes=[pltpu.VMEM((2, sdim // 2), jnp.float32), pltpu.VMEM((nch, sdim), jnp.float32),
                        pltpu.VMEM((nch, sdim), jnp.float32)],
        compiler_params=_cp(("parallel", "parallel", "arbitrary")), name="s5_scan",
    )(u, w_main, w_end, w_carry, lam_l)


def _router_kernel(h_ref, rw_ref, bias_ref, ut_ref, lt_ref, eid_ref, posk_ref, gatek_ref, cnt_ref, run_ref):
    ne = rw_ref.shape[0]
    tt = h_ref.shape[0]
    per = ne // N_GROUPS

    @pl.when(pl.program_id(0) == 0)
    def _():
        run_ref[...] = jnp.zeros_like(run_ref)

    s = _sigmoid(_dot_nt(rw_ref[...], h_ref[...]))
    bsc = s + bias_ref[...]
    b3 = bsc.reshape(N_GROUPS, per, tt)
    m1 = jnp.max(b3, axis=1, keepdims=True)
    is1 = b3 == m1
    n1 = jnp.sum(is1.astype(jnp.float32), axis=1, keepdims=True)
    m2 = jnp.max(jnp.where(is1, -jnp.inf, b3), axis=1, keepdims=True)
    gs = (m1 + jnp.where(n1 >= 2.0, m1, m2)).reshape(N_GROUPS, tt)
    gidx = lax.broadcasted_iota(jnp.int32, (N_GROUPS, tt), 0)
    grank = jnp.zeros((N_GROUPS, tt), jnp.float32)
    for g in range(N_GROUPS):
        row = gs[g:g + 1, :]
        ahead = (row > gs) | ((row == gs) & (g < gidx))
        grank = grank + ahead.astype(jnp.float32)
    gok = (grank < float(TOPK_GROUPS)).astype(jnp.float32)
    eok = jnp.broadcast_to(gok.reshape(N_GROUPS, 1, tt), (N_GROUPS, per, tt)).reshape(ne, tt)
    masked = jnp.where(eok > 0.0, bsc, -jnp.inf)
    eidx = lax.broadcasted_iota(jnp.int32, (ne, tt), 0)
    erank = jnp.zeros((ne, tt), jnp.float32)
    for e in range(ne):
        row = masked[e:e + 1, :]
        ahead = (row > masked) | ((row == masked) & (e < eidx))
        erank = erank + ahead.astype(jnp.float32)
    sel = erank < float(TOP_K)
    self32 = sel.astype(jnp.float32)
    selb = _bf16(self32)
    gsel = jnp.where(sel, s, 0.0)
    gsum = jnp.sum(gsel, axis=0, keepdims=True)
    gate = gsel / gsum * ROUTED_SCALE
    pos = _dot(selb, ut_ref[...]) + run_ref[...]
    run_ref[...] += jnp.sum(self32, axis=1, keepdims=True)
    cnt_ref[...] = jnp.broadcast_to(run_ref[...], cnt_ref.shape)
    slot = _dot(lt_ref[...], selb)
    eidf = eidx.astype(jnp.float32)
    for k in range(TOP_K):
        m = sel & (slot == float(k))
        eid_ref[k:k + 1, :] = jnp.sum(jnp.where(m, eidf, 0.0), axis=0, keepdims=True).astype(jnp.int32)
        posk_ref[k:k + 1, :] = jnp.sum(jnp.where(m, pos, 0.0), axis=0, keepdims=True).astype(jnp.int32)
        gatek_ref[k:k + 1, :] = jnp.sum(jnp.where(m, gate, 0.0), axis=0, keepdims=True)


def moe_router(hn, router_w, router_bias, tt=512):
    t, d = hn.shape
    ne = router_w.shape[1]
    tt = min(tt, t)
    r = lax.broadcasted_iota(jnp.int32, (tt, tt), 0)
    c = lax.broadcasted_iota(jnp.int32, (tt, tt), 1)
    ut = _bf16((r < c).astype(jnp.float32))
    re = lax.broadcasted_iota(jnp.int32, (ne, ne), 0)
    ce = lax.broadcasted_iota(jnp.int32, (ne, ne), 1)
    lt = _bf16((ce < re).astype(jnp.float32))
    slot_spec = pl.BlockSpec((TOP_K, tt), lambda i: (0, i))
    return pl.pallas_call(
        _router_kernel, grid=(t // tt,),
        in_specs=[
            pl.BlockSpec((tt, d), lambda i: (i, 0)),
            pl.BlockSpec((ne, d), lambda i: (0, 0)),
            pl.BlockSpec((ne, 1), lambda i: (0, 0)),
            pl.BlockSpec((tt, tt), lambda i: (0, 0)),
            pl.BlockSpec((ne, ne), lambda i: (0, 0)),
        ],
        out_specs=[slot_spec, slot_spec, slot_spec, pl.BlockSpec((ne, 128), lambda i: (0, 0))],
        out_shape=[jax.ShapeDtypeStruct((TOP_K, t), jnp.int32), jax.ShapeDtypeStruct((TOP_K, t), jnp.int32),
                   jax.ShapeDtypeStruct((TOP_K, t), jnp.float32), jax.ShapeDtypeStruct((ne, 128), jnp.float32)],
        scratch_shapes=[pltpu.VMEM((ne, 1), jnp.float32)],
        compiler_params=_cp(("arbitrary",)), name="moe_router",
    )(hn, _bf16(router_w.T), router_bias.reshape(ne, 1).astype(jnp.float32), ut, lt)


def _dest_kernel(start_ref, eid_ref, pos_ref, o_ref):
    eid = eid_ref[...]
    dest = pos_ref[...]
    for e in range(start_ref.shape[0]):
        dest = dest + jnp.where(eid == e, start_ref[e], 0)
    o_ref[...] = dest


def moe_dest(pad_start, eid, posk):
    return pl.pallas_call(
        _dest_kernel,
        grid_spec=pltpu.PrefetchScalarGridSpec(
            num_scalar_prefetch=1, grid=(1,),
            in_specs=[pl.BlockSpec(eid.shape, lambda i, s: (0, 0)), pl.BlockSpec(eid.shape, lambda i, s: (0, 0))],
            out_specs=pl.BlockSpec(eid.shape, lambda i, s: (0, 0))),
        out_shape=jax.ShapeDtypeStruct(eid.shape, jnp.int32),
        compiler_params=_cp(("arbitrary",)), name="moe_dest",
    )(pad_start, eid, posk)


def _dispatch_kernel(last_ref, dest_ref, src_ref, out_ref, zbuf, sem, zsem, *, tt):
    bm = zbuf.shape[0]

    @pl.when(pl.program_id(0) == 0)
    def _():
        zbuf[...] = jnp.zeros_like(zbuf)

        def zero_copy(e):
            return pltpu.make_async_copy(zbuf, out_ref.at[pl.ds(pl.multiple_of(last_ref[e], bm), bm)], zsem)

        def zstart(e, carry):
            @pl.when(last_ref[e] >= 0)
            def _():
                zero_copy(e).start()
            return carry

        def zwait(e, carry):
            @pl.when(last_ref[e] >= 0)
            def _():
                zero_copy(e).wait()
            return carry

        lax.fori_loop(0, last_ref.shape[0], zstart, 0)
        lax.fori_loop(0, last_ref.shape[0], zwait, 0)

    def body(t, carry):
        for k in range(TOP_K):
            pltpu.make_async_copy(src_ref.at[pl.ds(t, 1)], out_ref.at[pl.ds(dest_ref[k, t], 1)], sem).start()
        return carry

    lax.fori_loop(0, tt, body, 0)
    for k in range(TOP_K):
        pltpu.make_async_copy(src_ref, out_ref.at[pl.ds(0, tt)], sem).wait()


def moe_dispatch(zero_blocks, dest, hp, n_rows, bm, tt=256):
    t, w = hp.shape
    tt = min(tt, t)
    grid_spec = pltpu.PrefetchScalarGridSpec(
        num_scalar_prefetch=1, grid=(t // tt,),
        in_specs=[
            pl.BlockSpec((TOP_K, tt), lambda i, last: (0, i), memory_space=pltpu.SMEM),
            pl.BlockSpec((tt, w), lambda i, last: (i, 0)),
        ],
        out_specs=pl.BlockSpec(memory_space=pl.ANY),
        scratch_shapes=[pltpu.VMEM((bm, w), jnp.uint32), pltpu.SemaphoreType.DMA(()), pltpu.SemaphoreType.DMA(())],
    )
    return pl.pallas_call(
        functools.partial(_dispatch_kernel, tt=tt), grid_spec=grid_spec,
        out_shape=jax.ShapeDtypeStruct((n_rows, w), jnp.uint32),
        compiler_params=pltpu.CompilerParams(dimension_semantics=("arbitrary",), has_side_effects=True),
        name="moe_dispatch",
    )(zero_blocks, dest, hp)


def _expert_kernel(be_ref, nu_ref, x_ref, wg_ref, wu_ref, wd_ref, o_ref, wg_s, wu_s, wd_s):
    b = pl.program_id(0)
    e = be_ref[b]
    prev = be_ref[jnp.maximum(b - 1, 0)]
    half = x_ref.shape[1]

    @pl.when((b == 0) | (e != prev))
    def _():
        wg_s[0] = _bf16(wg_ref[0:half, :])
        wg_s[1] = _bf16(wg_ref[half:, :])
        wu_s[0] = _bf16(wu_ref[0:half, :])
        wu_s[1] = _bf16(wu_ref[half:, :])
        wd_s[...] = _bf16(wd_ref[...])

    @pl.when(b < nu_ref[0])
    def _():
        xp = x_ref[...]
        xa = _bf16(_unpack_hi(xp))
        xb = _bf16(_unpack_lo(xp))
        g = _dot(xa, wg_s[0]) + _dot(xb, wg_s[1])
        u = _dot(xa, wu_s[0]) + _dot(xb, wu_s[1])
        h = _bf16(g * _sigmoid(g) * u)
        y = _dot(h, wd_s[...])
        o_ref[...] = _pack_bf16_pair(y[:, :half], y[:, half:])

    @pl.when(b >= nu_ref[0])
    def _():
        o_ref[...] = jnp.zeros_like(o_ref)


def moe_experts(block_expert, n_used, xs, w_gate, w_up, w_down, layer, bm):
    n_rows, half = xs.shape
    _, ne, d, f = w_gate.shape
    nblk = n_rows // bm
    grid_spec = pltpu.PrefetchScalarGridSpec(
        num_scalar_prefetch=2, grid=(nblk,),
        in_specs=[
            pl.BlockSpec((bm, half), lambda b, be, nu: (jnp.minimum(b, nu[0] - 1), 0)),
            pl.BlockSpec((None, None, d, f), lambda b, be, nu: (layer, be[b], 0, 0)),
            pl.BlockSpec((None, None, d, f), lambda b, be, nu: (layer, be[b], 0, 0)),
            pl.BlockSpec((None, None, f, d), lambda b, be, nu: (layer, be[b], 0, 0)),
        ],
        out_specs=pl.BlockSpec((bm, half), lambda b, be, nu: (b, 0)),
        scratch_shapes=[pltpu.VMEM((2, half, f), jnp.bfloat16), pltpu.VMEM((2, half, f), jnp.bfloat16),
                        pltpu.VMEM((f, d), jnp.bfloat16)],
    )
    return pl.pallas_call(
        _expert_kernel, grid_spec=grid_spec,
        out_shape=jax.ShapeDtypeStruct((n_rows, half), jnp.uint32),
        compiler_params=_cp(("arbitrary",)), name="moe_experts",
    )(block_expert, n_used, xs, w_gate, w_up, w_down)


def _combine_kernel(dest_ref, nxt_ref, ys_ref, gk_ref, sh_ref, x_ref, ga_ref, o_ref, buf, sems, *, tt):
    i = pl.program_id(0)
    n = pl.num_programs(0)
    slot = i % 2

    def gather(d_ref, s):
        def body(t, carry):
            for k in range(TOP_K):
                pltpu.make_async_copy(ys_ref.at[pl.ds(d_ref[k, t], 1)], buf.at[s, k, pl.ds(t, 1)], sems.at[s]).start()
            return carry

        lax.fori_loop(0, tt, body, 0)

    @pl.when(i == 0)
    def _():
        gather(dest_ref, 0)

    @pl.when(i + 1 < n)
    def _():
        gather(nxt_ref, 1 - slot)

    for k in range(TOP_K):
        pltpu.make_async_copy(ys_ref.at[pl.ds(0, tt)], buf.at[slot, k], sems.at[slot]).wait()
    half = buf.shape[3]
    acc_a = jnp.zeros((tt, half), jnp.float32)
    acc_b = jnp.zeros((tt, half), jnp.float32)
    for k in range(TOP_K):
        p = buf[slot, k]
        gk = gk_ref[:, k:k + 1]
        acc_a = acc_a + gk * _unpack_hi(p)
        acc_b = acc_b + gk * _unpack_lo(p)
    sh = sh_ref[...].astype(jnp.float32)
    o_ref[:, 0:half] = x_ref[:, 0:half] + ga_ref[:, 0:half] * (acc_a + sh[:, 0:half])
    o_ref[:, half:] = x_ref[:, half:] + ga_ref[:, half:] * (acc_b + sh[:, half:])


def moe_combine(dest, ys, gates_tk, shared, x2d, gate_ada, seq, tt=128):
    t, d = x2d.shape
    half = ys.shape[1]
    tt = min(tt, seq)
    nt = t // tt
    return pl.pallas_call(
        functools.partial(_combine_kernel, tt=tt), grid=(nt,),
        in_specs=[
            pl.BlockSpec((TOP_K, tt), lambda i: (0, i), memory_space=pltpu.SMEM),
            pl.BlockSpec((TOP_K, tt), lambda i: (0, jnp.minimum(i + 1, nt - 1)), memory_space=pltpu.SMEM),
            pl.BlockSpec(memory_space=pl.ANY),
            pl.BlockSpec((tt, TOP_K), lambda i: (i, 0)),
            pl.BlockSpec((tt, d), lambda i: (i, 0)),
            pl.BlockSpec((tt, d), lambda i: (i, 0)),
            pl.BlockSpec((None, 1, d), lambda i: ((i * tt) // seq, 0, 0)),
        ],
        out_specs=pl.BlockSpec((tt, d), lambda i: (i, 0)),
        out_shape=jax.ShapeDtypeStruct((t, d), jnp.float32),
        scratch_shapes=[pltpu.VMEM((2, TOP_K, tt, half), jnp.uint32), pltpu.SemaphoreType.DMA((2,))],
        compiler_params=_cp(("arbitrary",)), name="moe_combine",
    )(dest, dest, ys, gates_tk, shared, x2d, gate_ada)


MOE_ROWS = 256


def moe_layer(x2d, hn, hp, gate_ada, seq, router_w, router_bias, w_gate, w_up, w_down, layer, s_gate, s_up, s_down):
    t, d = x2d.shape
    ne = router_w.shape[1]
    bm = min(MOE_ROWS, t)
    eid, posk, gates_k, cnt = moe_router(hn, router_w, router_bias)
    counts = cnt[:, 0].astype(jnp.int32)
    padded = (counts + bm - 1) // bm * bm
    pad_end = jnp.cumsum(padded)
    pad_start = pad_end - padded
    n_rows = t * TOP_K + ne * bm
    nblk = n_rows // bm
    dest = moe_dest(pad_start.astype(jnp.int32), eid, posk)
    block_row = jnp.arange(nblk, dtype=jnp.int32) * bm
    block_expert = jnp.minimum(jnp.sum(pad_end[None, :] <= block_row[:, None], axis=1), ne - 1).astype(jnp.int32)
    n_used = (pad_end[-1:] // bm).astype(jnp.int32)
    tail = (n_used[0] + jnp.arange(ne, dtype=jnp.int32)) * bm
    zero_blocks = jnp.concatenate([jnp.where(padded > counts, pad_end - bm, -1),
                                   jnp.where(tail < n_rows, tail, -1)]).astype(jnp.int32)
    xs = moe_dispatch(zero_blocks, dest, hp, n_rows, bm)
    ys = moe_experts(block_expert, n_used, xs, w_gate, w_up, w_down, layer, bm)
    hs = matmul_swiglu(hn, _bf16(s_gate), _bf16(s_up))
    shared = matmul(hs, _bf16(s_down), jnp.bfloat16)
    return moe_combine(dest, ys, gates_k.T, shared, x2d, gate_ada, seq)


def kernel(x, c, mix_norm_g, mix_ada_w, mix_ada_b, ev_w_in, ev_conv_w, ev_conv_b, ev_ln_g, ev_ln_b, ev_q_norm_g, ev_k_norm_g, ev_w_out, od_w_in, od_log_dt, od_lambda_re, od_lambda_im, od_b_re, od_b_im, od_c_re, od_c_im, od_d, od_w_out, ffn_norm_g, ffn_ada_w, ffn_ada_b, router_w, router_bias, exp_gate, exp_up, exp_down, sh_gate, sh_up, sh_down):
    bsz, seq, d = x.shape
    depth = mix_norm_g.shape[0]
    t = bsz * seq
    c_pad = jnp.zeros((8, d), jnp.float32).at[:bsz].set(c)
    mix_mod = adaln_all(c_pad, mix_ada_w, mix_ada_b)[:, :bsz]
    ffn_mod = adaln_all(c_pad, ffn_ada_w, ffn_ada_b)[:, :bsz]

    def split(mod):
        return tuple(mod[:, None, k * d:(k + 1) * d] for k in range(3))

    for i in range(depth):
        j = i // 2
        shift, scale, gate = split(mix_mod[i])
        hn = norm_modulate(x, mix_norm_g[i], scale, shift)
        x2d = x.reshape(t, d)
        if i % 2 == 0:
            cc = ev_conv_w.shape[-1]
            dh = ev_q_norm_g.shape[-1]
            sbw = (ev_w_in.shape[-1] - 2 * cc) // 3
            proj = matmul(hn, _bf16(ev_w_in[j]), jnp.bfloat16)
            a = conv_branch(proj, ev_conv_w[j], ev_conv_b[j], ev_ln_g[j], ev_ln_b[j], seq)
            qn, kn = qk_norm(proj, ev_q_norm_g[j], ev_k_norm_g[j], 2 * cc, sbw)
            o = stickbreak_attention(qn, kn, proj, 2 * cc + 2 * sbw, bsz, seq, dh)
            x2d = matmul2_resid(a, o, _bf16(ev_w_out[j]), x2d, gate, seq)
        else:
            u = matmul(hn, _bf16(od_w_in[j]), jnp.float32)
            wts = s5_weights(od_log_dt[j], od_lambda_re[j], od_lambda_im[j], od_b_re[j], od_b_im[j],
                             od_c_re[j], od_c_im[j], od_d[j])
            y = s5_scan(u.reshape(bsz, seq, -1), *wts)
            x2d = matmul_glu_resid(y.reshape(t, -1), _bf16(od_w_out[j]), x2d, gate, seq)
        x = x2d.reshape(bsz, seq, d)
        shift, scale, gate = split(ffn_mod[i])
        hn, hp = norm_modulate(x, ffn_norm_g[i], scale, shift, packed=True)
        x2d = moe_layer(x.reshape(t, d), hn, hp, gate, seq, router_w[i], router_bias[i], exp_gate, exp_up,
                        exp_down, i, sh_gate[i], sh_up[i], sh_down[i])
        x = x2d.reshape(bsz, seq, d)
    return x
```

```python
import functools
import math

import jax
import jax.numpy as jnp
from jax import lax
from jax.experimental import pallas as pl
from jax.experimental.pallas import tpu as pltpu

EPS = 1e-6
TOP_K = 8
N_GROUPS = 8
TOPK_GROUPS = 4
ROUTED_SCALE = 2.5
S5_CHUNK = 8
S5_BLOCK_LANES = 128

V7X_VMEM_LIMIT = 56 * 1024 * 1024


def _cp(dims, vmem=V7X_VMEM_LIMIT):
    return pltpu.CompilerParams(dimension_semantics=dims, vmem_limit_bytes=vmem)


def _sigmoid(x):
    return 1.0 / (1.0 + jnp.exp(-x))


def _bf16(x):
    return x.astype(jnp.bfloat16)


def _dot(a, b):
    return jnp.dot(a, b, preferred_element_type=jnp.float32)


def _tile(n, target, quantum=128):
    if n <= target:
        return n
    best = quantum
    for cand in range(quantum, target + 1, quantum):
        if n % cand == 0:
            best = cand
    assert n % best == 0, (n, target)
    return best


def _dot_nt(a, b):
    return lax.dot_general(a, b, (((1,), (1,)), ((), ())), preferred_element_type=jnp.float32)


def _adaln_kernel(c_ref, w_ref, b_ref, o_ref):
    c = c_ref[...]
    sc = _bf16(c * _sigmoid(c))
    o_ref[...] = _dot(sc, _bf16(w_ref[...])) + b_ref[...]


def adaln_all(c_pad, w, b, tn=512):
    nl, d, n = w.shape
    tn = min(tn, n)
    return pl.pallas_call(
        _adaln_kernel,
        grid=(nl, n // tn),
        in_specs=[
            pl.BlockSpec((8, d), lambda l, j: (0, 0)),
            pl.BlockSpec((None, d, tn), lambda l, j: (l, 0, j)),
            pl.BlockSpec((None, 1, tn), lambda l, j: (l, 0, j)),
        ],
        out_specs=pl.BlockSpec((None, 8, tn), lambda l, j: (l, 0, j)),
        out_shape=jax.ShapeDtypeStruct((nl, 8, n), jnp.float32),
        compiler_params=_cp(("parallel", "parallel")),
        name="adaln",
    )(c_pad, w, b.reshape(nl, 1, n))


def _pack_bf16_pair(a, b):
    ua = pltpu.bitcast(_bf16(a).astype(jnp.float32), jnp.uint32)
    ub = pltpu.bitcast(_bf16(b).astype(jnp.float32), jnp.uint32)
    return (ua & jnp.uint32(0xFFFF0000)) | (ub >> 16)


def _unpack_hi(p):
    return pltpu.bitcast(p & jnp.uint32(0xFFFF0000), jnp.float32)


def _unpack_lo(p):
    return pltpu.bitcast(p << 16, jnp.float32)


def _norm_mod(x, g_ref, sc_ref, sh_ref):
    xf = x.astype(jnp.float32)
    ms = jnp.mean(xf * xf, axis=-1, keepdims=True)
    return xf * lax.rsqrt(ms + EPS) * g_ref[...] * (1.0 + sc_ref[...]) + sh_ref[...]


def _norm_kernel(x_ref, g_ref, sc_ref, sh_ref, o_ref):
    o_ref[...] = _bf16(_norm_mod(x_ref[...], g_ref, sc_ref, sh_ref))


def _norm_pack_kernel(x_ref, g_ref, sc_ref, sh_ref, o_ref, p_ref):
    y = _norm_mod(x_ref[...], g_ref, sc_ref, sh_ref)
    o_ref[...] = _bf16(y)
    h = y.shape[-1] // 2
    p_ref[...] = _pack_bf16_pair(y[:, :h], y[:, h:])


def norm_modulate(x, g, scale, shift, packed=False, ts=256):
    b, s, d = x.shape
    ts = min(ts, s)
    nt = s // ts
    in_specs = [
        pl.BlockSpec((None, ts, d), lambda i, j: (i, j, 0)),
        pl.BlockSpec((1, d), lambda i, j: (0, 0)),
        pl.BlockSpec((None, 1, d), lambda i, j: (i, 0, 0)),
        pl.BlockSpec((None, 1, d), lambda i, j: (i, 0, 0)),
    ]
    o_spec = pl.BlockSpec((ts, d), lambda i, j: (i * nt + j, 0))
    o_shape = jax.ShapeDtypeStruct((b * s, d), jnp.bfloat16)
    if not packed:
        return pl.pallas_call(
            _norm_kernel, grid=(b, nt), in_specs=in_specs, out_specs=o_spec, out_shape=o_shape,
            compiler_params=_cp(("parallel", "parallel")), name="norm_mod",
        )(x, g.reshape(1, d), scale, shift)
    return pl.pallas_call(
        _norm_pack_kernel, grid=(b, nt), in_specs=in_specs,
        out_specs=[o_spec, pl.BlockSpec((ts, d // 2), lambda i, j: (i * nt + j, 0))],
        out_shape=[o_shape, jax.ShapeDtypeStruct((b * s, d // 2), jnp.uint32)],
        compiler_params=_cp(("parallel", "parallel")), name="norm_mod_pack",
    )(x, g.reshape(1, d), scale, shift)


def _mm_kernel(a_ref, w_ref, o_ref):
    o_ref[...] = _dot(a_ref[...], w_ref[...]).astype(o_ref.dtype)


def matmul(a, w, out_dtype, bm=1024, bn=1024):
    m, k = a.shape
    n = w.shape[1]
    bm, bn = _tile(m, bm), _tile(n, bn)
    return pl.pallas_call(
        _mm_kernel, grid=(m // bm, n // bn),
        in_specs=[pl.BlockSpec((bm, k), lambda i, j: (i, 0)), pl.BlockSpec((k, bn), lambda i, j: (0, j))],
        out_specs=pl.BlockSpec((bm, bn), lambda i, j: (i, j)),
        out_shape=jax.ShapeDtypeStruct((m, n), out_dtype),
        compiler_params=_cp(("parallel", "parallel")), name="matmul",
    )(a, w)


def _mm2_resid_kernel(a1_ref, a2_ref, w1_ref, w2_ref, x_ref, g_ref, o_ref):
    m = _dot(a1_ref[...], w1_ref[...]) + _dot(a2_ref[...], w2_ref[...])
    o_ref[...] = x_ref[...] + g_ref[...] * m


def matmul2_resid(a1, a2, w, x2d, gate, seq, bm=1024, bn=1024):
    m, k1 = a1.shape
    k2 = a2.shape[1]
    assert k1 == k2
    n = w.shape[1]
    bm, bn = _tile(seq, bm), _tile(n, bn)
    return pl.pallas_call(
        _mm2_resid_kernel, grid=(m // bm, n // bn),
        in_specs=[
            pl.BlockSpec((bm, k1), lambda i, j: (i, 0)),
            pl.BlockSpec((bm, k2), lambda i, j: (i, 0)),
            pl.BlockSpec((k1, bn), lambda i, j: (0, j)),
            pl.BlockSpec((k2, bn), lambda i, j: (1, j)),
            pl.BlockSpec((bm, bn), lambda i, j: (i, j)),
            pl.BlockSpec((None, 1, bn), lambda i, j: ((i * bm) // seq, 0, j)),
        ],
        out_specs=pl.BlockSpec((bm, bn), lambda i, j: (i, j)),
        out_shape=jax.ShapeDtypeStruct((m, n), jnp.float32),
        compiler_params=_cp(("parallel", "parallel")), name="matmul2_resid",
    )(a1, a2, w, w, x2d, gate)


def _mm_glu_resid_kernel(a_ref, wa_ref, wb_ref, x_ref, g_ref, o_ref):
    a = a_ref[...]
    va = _dot(a, wa_ref[...])
    vb = _dot(a, wb_ref[...])
    o_ref[...] = x_ref[...] + g_ref[...] * (va * _sigmoid(vb))


def matmul_glu_resid(a, w, x2d, gate, seq, bm=1024, bn=512):
    m, k = a.shape
    n = w.shape[1] // 2
    bm, bn = _tile(seq, bm), _tile(n, bn)
    nb = n // bn
    return pl.pallas_call(
        _mm_glu_resid_kernel, grid=(m // bm, nb),
        in_specs=[
            pl.BlockSpec((bm, k), lambda i, j: (i, 0)),
            pl.BlockSpec((k, bn), lambda i, j: (0, j)),
            pl.BlockSpec((k, bn), lambda i, j: (0, j + nb)),
            pl.BlockSpec((bm, bn), lambda i, j: (i, j)),
            pl.BlockSpec((None, 1, bn), lambda i, j: ((i * bm) // seq, 0, j)),
        ],
        out_specs=pl.BlockSpec((bm, bn), lambda i, j: (i, j)),
        out_shape=jax.ShapeDtypeStruct((m, n), jnp.float32),
        compiler_params=_cp(("parallel", "parallel")), name="matmul_glu_resid",
    )(a, w, w, x2d, gate)


def _mm_swiglu_kernel(a_ref, wg_ref, wu_ref, o_ref):
    a = a_ref[...]
    g = _dot(a, wg_ref[...])
    u = _dot(a, wu_ref[...])
    o_ref[...] = _bf16(g * _sigmoid(g) * u)


def matmul_swiglu(a, wg, wu, bm=1024, bn=512):
    m, k = a.shape
    n = wg.shape[1]
    bm, bn = _tile(m, bm), _tile(n, bn)
    return pl.pallas_call(
        _mm_swiglu_kernel, grid=(m // bm, n // bn),
        in_specs=[
            pl.BlockSpec((bm, k), lambda i, j: (i, 0)),
            pl.BlockSpec((k, bn), lambda i, j: (0, j)),
            pl.BlockSpec((k, bn), lambda i, j: (0, j)),
        ],
        out_specs=pl.BlockSpec((bm, bn), lambda i, j: (i, j)),
        out_shape=jax.ShapeDtypeStruct((m, n), jnp.bfloat16),
        compiler_params=_cp(("parallel", "parallel")), name="matmul_swiglu",
    )(a, wg, wu)


CONV_HALO = 32
CONV_ROWS = 64
CONV_LANES = 256


def _conv_kernel(v_ref, g_ref, vh_ref, gh_ref, w_ref, cb_ref, lg_ref, lb_ref, o_ref, abuf, cbuf, *,
                 tiles_per_seq, width):
    ts, cc = o_ref.shape
    first = (pl.program_id(0) % tiles_per_seq) == 0
    vh = vh_ref[...].astype(jnp.float32)
    gh = gh_ref[...].astype(jnp.float32)
    abuf[0:CONV_HALO, :] = jnp.where(first, 0.0, vh * _sigmoid(gh))
    v = v_ref[...].astype(jnp.float32)
    g = g_ref[...].astype(jnp.float32)
    abuf[CONV_HALO:, :] = v * _sigmoid(g)
    off = CONV_HALO - (width - 1)
    for lc in range(cc // CONV_LANES):
        ls = slice(lc * CONV_LANES, (lc + 1) * CONV_LANES)
        for rc in range(ts // CONV_ROWS):
            r0 = rc * CONV_ROWS
            acc = jnp.zeros((CONV_ROWS, CONV_LANES), jnp.float32)
            for j in range(width):
                acc = acc + w_ref[j:j + 1, ls] * abuf[r0 + off + j:r0 + off + j + CONV_ROWS, ls]
            cbuf[r0:r0 + CONV_ROWS, ls] = acc + cb_ref[:, ls]
    c = cbuf[...]
    mu = jnp.mean(c, axis=-1, keepdims=True)
    cen = c - mu
    var = jnp.mean(cen * cen, axis=-1, keepdims=True)
    y = cen * lax.rsqrt(var + EPS) * lg_ref[...] + lb_ref[...]
    o_ref[...] = _bf16(y * _sigmoid(y))


def conv_branch(proj, conv_w, conv_b, ln_g, ln_b, seq, ts=128):
    t = proj.shape[0]
    width, cc = conv_w.shape
    ts = min(ts, seq)
    assert width - 1 <= CONV_HALO and ts % CONV_HALO == 0 and cc % CONV_LANES == 0 and ts % CONV_ROWS == 0
    r = ts // CONV_HALO
    halo_idx = lambda i: jnp.maximum(i * r - 1, 0)
    kern = functools.partial(_conv_kernel, tiles_per_seq=seq // ts, width=width)
    return pl.pallas_call(
        kern, grid=(t // ts,),
        in_specs=[
            pl.BlockSpec((ts, cc), lambda i: (i, 0)),
            pl.BlockSpec((ts, cc), lambda i: (i, 1)),
            pl.BlockSpec((CONV_HALO, cc), lambda i: (halo_idx(i), 0)),
            pl.BlockSpec((CONV_HALO, cc), lambda i: (halo_idx(i), 1)),
            pl.BlockSpec((width, cc), lambda i: (0, 0)),
            pl.BlockSpec((1, cc), lambda i: (0, 0)),
            pl.BlockSpec((1, cc), lambda i: (0, 0)),
            pl.BlockSpec((1, cc), lambda i: (0, 0)),
        ],
        out_specs=pl.BlockSpec((ts, cc), lambda i: (i, 0)),
        out_shape=jax.ShapeDtypeStruct((t, cc), jnp.bfloat16),
        scratch_shapes=[pltpu.VMEM((CONV_HALO + ts, cc), jnp.float32), pltpu.VMEM((ts, cc), jnp.float32)],
        compiler_params=_cp(("parallel",)), name="conv_branch",
    )(proj, proj, proj, proj, conv_w, conv_b.reshape(1, cc), ln_g.reshape(1, cc), ln_b.reshape(1, cc))


def _qknorm_kernel(q_ref, k_ref, gq_ref, gk_ref, qo_ref, ko_ref, *, dh):
    q_scale = math.log2(math.e) / math.sqrt(dh)
    for src, g_ref, dst, mul in ((q_ref, gq_ref, qo_ref, q_scale), (k_ref, gk_ref, ko_ref, 1.0)):
        for h in range(src.shape[1] // dh):
            ls = slice(h * dh, (h + 1) * dh)
            xf = src[:, ls].astype(jnp.float32)
            ms = jnp.mean(xf * xf, axis=-1, keepdims=True)
            dst[:, ls] = _bf16(xf * lax.rsqrt(ms + EPS) * g_ref[...] * mul)


def qk_norm(proj, gq, gk, col0, width, ts=512):
    t = proj.shape[0]
    dh = gq.shape[0]
    ts = min(ts, t)
    qb, kb = col0 // width, col0 // width + 1
    return pl.pallas_call(
        functools.partial(_qknorm_kernel, dh=dh), grid=(t // ts,),
        in_specs=[
            pl.BlockSpec((ts, width), lambda i: (i, qb)),
            pl.BlockSpec((ts, width), lambda i: (i, kb)),
            pl.BlockSpec((1, dh), lambda i: (0, 0)),
            pl.BlockSpec((1, dh), lambda i: (0, 0)),
        ],
        out_specs=[pl.BlockSpec((ts, width), lambda i: (i, 0)), pl.BlockSpec((ts, width), lambda i: (i, 0))],
        out_shape=[jax.ShapeDtypeStruct((t, width), jnp.bfloat16)] * 2,
        compiler_params=_cp(("parallel",)), name="qk_norm",
    )(proj, proj, gq.reshape(1, dh), gk.reshape(1, dh))


ATT_TILE = 1024
ATT_SUB = 256


def _attn_kernel(qi_ref, kj_ref, q_ref, k_ref, v_ref, u_ref, m_ref, o_ref, acc_ref, r_ref):
    p = pl.program_id(2)
    qi = qi_ref[p]
    kj = kj_ref[p]
    tq = q_ref.shape[0]
    tk = k_ref.shape[0]
    sub = u_ref.shape[1]

    @pl.when(kj == qi)
    def _():
        acc_ref[...] = jnp.zeros_like(acc_ref)
        r_ref[...] = jnp.zeros_like(r_ref)

    def process(diagonal):
        nsb = tk // sub
        z = _dot_nt(q_ref[...], k_ref[...])
        zb = _bf16(z)
        one = jnp.ones((), jnp.bfloat16)
        log2e = jnp.asarray(math.log2(math.e), jnp.bfloat16)
        lk = -(jnp.maximum(zb, 0) + jnp.log(one + jnp.exp2(-jnp.abs(zb))) * log2e)
        lm = lk * m_ref[...] if diagonal else lk
        r = r_ref[...]
        acc = acc_ref[...]
        for s in reversed(range(nsb)):
            ks = slice(s * sub, (s + 1) * sub)
            cs = _dot(lm[:, ks], u_ref[...])
            a = _bf16(jnp.exp2(z[:, ks] + lk[:, ks].astype(jnp.float32) + (cs + r)))
            if diagonal:
                a = a * m_ref[:, ks]
            acc = acc + _dot(a, v_ref[ks, :])
            r = r + cs[:, 0:1] + lm[:, s * sub:s * sub + 1].astype(jnp.float32)
        r_ref[...] = r
        acc_ref[...] = acc

    @pl.when(kj == qi)
    def _():
        process(True)

    @pl.when(kj != qi)
    def _():
        process(False)

    @pl.when(kj == 0)
    def _():
        o_ref[...] = acc_ref[...].astype(o_ref.dtype)


def stickbreak_attention(qn, kn, proj, vcol0, batch, seq, dh):
    t, w = qn.shape
    heads = w // dh
    tile = min(ATT_TILE, seq)
    sub = min(ATT_SUB, tile)
    nq = seq // tile
    pairs = [(i, j) for i in range(nq) for j in range(i, -1, -1)]
    qi = jnp.asarray([p[0] for p in pairs], jnp.int32)
    kj = jnp.asarray([p[1] for p in pairs], jnp.int32)
    r = lax.broadcasted_iota(jnp.int32, (sub, sub), 0)
    c = lax.broadcasted_iota(jnp.int32, (sub, sub), 1)
    tri = _bf16((r > c).astype(jnp.float32))
    rt = lax.broadcasted_iota(jnp.int32, (tile, tile), 0)
    ct = lax.broadcasted_iota(jnp.int32, (tile, tile), 1)
    causal = _bf16((ct < rt).astype(jnp.float32))
    vb = vcol0 // dh
    grid_spec = pltpu.PrefetchScalarGridSpec(
        num_scalar_prefetch=2,
        grid=(batch, heads, len(pairs)),
        in_specs=[
            pl.BlockSpec((tile, dh), lambda b, h, p, qi, kj: (b * nq + qi[p], h)),
            pl.BlockSpec((tile, dh), lambda b, h, p, qi, kj: (b * nq + kj[p], h)),
            pl.BlockSpec((tile, dh), lambda b, h, p, qi, kj: (b * nq + kj[p], vb + h)),
            pl.BlockSpec((sub, sub), lambda b, h, p, qi, kj: (0, 0)),
            pl.BlockSpec((tile, tile), lambda b, h, p, qi, kj: (0, 0)),
        ],
        out_specs=pl.BlockSpec((tile, dh), lambda b, h, p, qi, kj: (b * nq + qi[p], h)),
        scratch_shapes=[pltpu.VMEM((tile, dh), jnp.float32), pltpu.VMEM((tile, 1), jnp.float32)],
    )
    return pl.pallas_call(
        _attn_kernel, grid_spec=grid_spec,
        out_shape=jax.ShapeDtypeStruct((t, w), jnp.bfloat16),
        compiler_params=_cp(("parallel", "parallel", "arbitrary")), name="stickbreak_attn",
    )(qi, kj, qn, kn, proj, tri, causal)


def _gelu_tanh(x):
    return 0.5 * x * (1.0 + jnp.tanh(math.sqrt(2.0 / math.pi) * (x + 0.044715 * (x * x * x))))


def _s5_kernel(u_ref, wm_ref, we_ref, wc_ref, lam_ref, y_ref, s_ref, e_ref, sp_ref, yb_ref):
    tt, bl = u_ref.shape
    nch = tt // S5_CHUNK
    half = s_ref.shape[1]

    @pl.when(pl.program_id(2) == 0)
    def _():
        s_ref[...] = jnp.zeros_like(s_ref)

    ucat = _bf16(jnp.concatenate(
        [u_ref[pl.ds(j, nch, stride=S5_CHUNK), :] for j in range(S5_CHUNK)], axis=1))
    e_ref[...] = _dot(ucat, we_ref[...])
    lam_re = lam_ref[0:1, :]
    lam_im = lam_ref[1:2, :]

    def body(c, carry):
        sre, sim = carry
        sp_ref[pl.ds(c, 1), 0:half] = sre
        sp_ref[pl.ds(c, 1), half:] = sim
        ere = e_ref[pl.ds(c, 1), 0:half]
        eim = e_ref[pl.ds(c, 1), half:]
        return (lam_re * sre - lam_im * sim + ere, lam_re * sim + lam_im * sre + eim)

    sre, sim = lax.fori_loop(0, nch, body, (s_ref[0:1, :], s_ref[1:2, :]))
    s_ref[0:1, :] = sre
    s_ref[1:2, :] = sim
    ycat = _dot(ucat, wm_ref[...]) + _dot(_bf16(sp_ref[...]), wc_ref[...])
    for j in range(S5_CHUNK):
        yb_ref[pl.ds(j, nch, stride=S5_CHUNK), :] = _gelu_tanh(ycat[:, j * bl:(j + 1) * bl])
    y_ref[...] = _bf16(yb_ref[...])


def _s5_expand_kernel(c_ref, e_ref, m_ref, o_ref):
    o_ref[...] = _bf16(_dot(c_ref[...], e_ref[...])) * m_ref[...]


def s5_expand(compact, spread, same_group):
    nb, rows, cc = compact.shape
    cols = spread.shape[1]
    return pl.pallas_call(
        _s5_expand_kernel, grid=(nb,),
        in_specs=[pl.BlockSpec((None, rows, cc), lambda b: (b, 0, 0)),
                  pl.BlockSpec((cc, cols), lambda b: (0, 0)),
                  pl.BlockSpec((rows, cols), lambda b: (0, 0))],
        out_specs=pl.BlockSpec((None, rows, cols), lambda b: (b, 0, 0)),
        out_shape=jax.ShapeDtypeStruct((nb, rows, cols), jnp.bfloat16),
        compiler_params=_cp(("parallel",)), name="s5_expand",
    )(compact, spread, same_group)


def s5_weights(log_dt, lam_re, lam_im, b_re, b_im, c_re, c_im, d_skip):
    g, p = lam_re.shape
    hg = b_re.shape[-1]
    L = S5_CHUNK
    gb = S5_BLOCK_LANES // hg
    nb = g // gb
    dt = jnp.exp(log_dt)[:, None]
    mag = jnp.exp(lam_re * dt)
    ang = lam_im * dt
    lbar_re = mag * jnp.cos(ang)
    lbar_im = mag * jnp.sin(ang)
    den = lam_re * lam_re + lam_im * lam_im
    nr = lbar_re - 1.0
    f_re = (nr * lam_re + lbar_im * lam_im) / den
    f_im = (lbar_im * lam_re - nr * lam_im) / den
    bbar_re = f_re[..., None] * b_re - f_im[..., None] * b_im
    bbar_im = f_re[..., None] * b_im + f_im[..., None] * b_re
    n = jnp.arange(L + 1, dtype=jnp.float32)[:, None, None]
    pw_mag = jnp.exp(n * (lam_re * dt)[None])
    pw_re = pw_mag * jnp.cos(n * ang[None])
    pw_im = pw_mag * jnp.sin(n * ang[None])
    lb_re = pw_re[..., None] * bbar_re[None] - pw_im[..., None] * bbar_im[None]
    lb_im = pw_re[..., None] * bbar_im[None] + pw_im[..., None] * bbar_re[None]
    ktau = (jnp.einsum('gop,ngpi->ngoi', c_re, lb_re[:L], precision='highest')
            - jnp.einsum('gop,ngpi->ngoi', c_im, lb_im[:L], precision='highest'))
    jj = jnp.arange(L)[:, None]
    tt = jnp.arange(L)[None, :]
    tau = jnp.clip(tt - jj, 0, L - 1)
    kjt = jnp.where((tt >= jj)[..., None, None, None], ktau[tau], 0.0)
    kjt = kjt + (jj == tt)[..., None, None, None] * (d_skip[:, :, None] * jnp.eye(hg))[None, None]

    def block_diag(small):
        nb_, r0, g_, r1, c0, c1 = small.shape
        rows, cc, cols = r0 * g_ * r1, c0 * c1, c0 * g_ * c1
        ri = jnp.arange(rows)[:, None]
        ki = jnp.arange(cc)[:, None]
        ci = jnp.arange(cols)[None, :]
        spread = (ki // c1 == ci // (g_ * c1)) & (ki % c1 == ci % c1)
        same_group = (ri // r1) % g_ == (ci // c1) % g_
        return s5_expand(_bf16(small.reshape(nb_, rows, cc)), _bf16(spread.astype(jnp.float32)),
                         _bf16(same_group.astype(jnp.float32)))

    w_main = block_diag(kjt.reshape(L, L, nb, gb, hg, hg).transpose(2, 0, 3, 5, 1, 4))
    le_re = lb_re[L - 1 - jnp.arange(L)].reshape(L, nb, gb, p, hg)
    le_im = lb_im[L - 1 - jnp.arange(L)].reshape(L, nb, gb, p, hg)
    le = jnp.stack([le_re, le_im], axis=0)
    w_end = block_diag(le.transpose(2, 1, 3, 5, 0, 4))
    m_re = c_re[None] * pw_re[1:, :, None, :] - c_im[None] * pw_im[1:, :, None, :]
    m_im = c_re[None] * pw_im[1:, :, None, :] + c_im[None] * pw_re[1:, :, None, :]
    mc = jnp.stack([m_re, -m_im], axis=0).reshape(2, L, nb, gb, hg, p)
    w_carry = block_diag(mc.transpose(2, 0, 3, 5, 1, 4))
    lam_l = jnp.stack([pw_re[L].reshape(nb, gb * p), pw_im[L].reshape(nb, gb * p)], axis=1)
    return w_main, w_end, w_carry, lam_l


def s5_scan(u, w_main, w_end, w_carry, lam_l, tt=2048):
    b, s, w = u.shape
    nb, kdim, sdim = w_end.shape
    bl = S5_BLOCK_LANES
    tt = min(tt, s)
    nch = tt // S5_CHUNK
    return pl.pallas_call(
        _s5_kernel, grid=(b, nb, s // tt),
        in_specs=[
            pl.BlockSpec((None, tt, bl), lambda i, j, t: (i, t, j)),
            pl.BlockSpec((None, kdim, kdim), lambda i, j, t: (j, 0, 0)),
            pl.BlockSpec((None, kdim, sdim), lambda i, j, t: (j, 0, 0)),
            pl.BlockSpec((None, sdim, kdim), lambda i, j, t: (j, 0, 0)),
            pl.BlockSpec((None, 2, sdim // 2), lambda i, j, t: (j, 0, 0)),
        ],
        out_specs=pl.BlockSpec((None, tt, bl), lambda i, j, t: (i, t, j)),
        out_shape=jax.ShapeDtypeStruct((b, s, w), jnp.bfloat16),
        scratch_shapes=[pltpu.VMEM((2, sdim // 2), jnp.float32), pltpu.VMEM((nch, sdim), jnp.float32),
                        pltpu.VMEM((nch, sdim), jnp.float32), pltpu.VMEM((tt, bl), jnp.float32)],
        compiler_params=_cp(("parallel", "parallel", "arbitrary")), name="s5_scan",
    )(u, w_main, w_end, w_carry, lam_l)


def _router_kernel(h_ref, rw_ref, bias_ref, ut_ref, lt_ref, eid_ref, posk_ref, gatek_ref, cnt_ref, run_ref):
    ne = rw_ref.shape[0]
    tt = h_ref.shape[0]
    per = ne // N_GROUPS

    @pl.when(pl.program_id(0) == 0)
    def _():
        run_ref[...] = jnp.zeros_like(run_ref)

    s = _sigmoid(_dot_nt(rw_ref[...], h_ref[...]))
    bsc = s + bias_ref[...]
    b3 = bsc.reshape(N_GROUPS, per, tt)
    m1 = jnp.max(b3, axis=1, keepdims=True)
    is1 = b3 == m1
    n1 = jnp.sum(is1.astype(jnp.float32), axis=1, keepdims=True)
    m2 = jnp.max(jnp.where(is1, -jnp.inf, b3), axis=1, keepdims=True)
    gs = (m1 + jnp.where(n1 >= 2.0, m1, m2)).reshape(N_GROUPS, tt)
    gidx = lax.broadcasted_iota(jnp.int32, (N_GROUPS, tt), 0)
    grank = jnp.zeros((N_GROUPS, tt), jnp.float32)
    for g in range(N_GROUPS):
        row = gs[g:g + 1, :]
        ahead = (row > gs) | ((row == gs) & (g < gidx))
        grank = grank + ahead.astype(jnp.float32)
    gok = (grank < float(TOPK_GROUPS)).astype(jnp.float32)
    eok = jnp.broadcast_to(gok.reshape(N_GROUPS, 1, tt), (N_GROUPS, per, tt)).reshape(ne, tt)
    masked = jnp.where(eok > 0.0, bsc, -jnp.inf)
    eidx = lax.broadcasted_iota(jnp.int32, (ne, tt), 0)
    erank = jnp.zeros((ne, tt), jnp.float32)
    for e in range(ne):
        row = masked[e:e + 1, :]
        ahead = (row > masked) | ((row == masked) & (e < eidx))
        erank = erank + ahead.astype(jnp.float32)
    sel = erank < float(TOP_K)
    self32 = sel.astype(jnp.float32)
    selb = _bf16(self32)
    gsel = jnp.where(sel, s, 0.0)
    gsum = jnp.sum(gsel, axis=0, keepdims=True)
    gate = gsel / gsum * ROUTED_SCALE
    pos = _dot(selb, ut_ref[...]) + run_ref[...]
    run_ref[...] += jnp.sum(self32, axis=1, keepdims=True)
    cnt_ref[...] = jnp.broadcast_to(run_ref[...], cnt_ref.shape)
    slot = _dot(lt_ref[...], selb)
    eidf = eidx.astype(jnp.float32)
    for k in range(TOP_K):
        m = sel & (slot == float(k))
        eid_ref[k:k + 1, :] = jnp.sum(jnp.where(m, eidf, 0.0), axis=0, keepdims=True).astype(jnp.int32)
        posk_ref[k:k + 1, :] = jnp.sum(jnp.where(m, pos, 0.0), axis=0, keepdims=True).astype(jnp.int32)
        gatek_ref[k:k + 1, :] = jnp.sum(jnp.where(m, gate, 0.0), axis=0, keepdims=True)


def moe_router(hn, router_w, router_bias, tt=512):
    t, d = hn.shape
    ne = router_w.shape[1]
    tt = min(tt, t)
    r = lax.broadcasted_iota(jnp.int32, (tt, tt), 0)
    c = lax.broadcasted_iota(jnp.int32, (tt, tt), 1)
    ut = _bf16((r < c).astype(jnp.float32))
    re = lax.broadcasted_iota(jnp.int32, (ne, ne), 0)
    ce = lax.broadcasted_iota(jnp.int32, (ne, ne), 1)
    lt = _bf16((ce < re).astype(jnp.float32))
    slot_spec = pl.BlockSpec((TOP_K, tt), lambda i: (0, i))
    return pl.pallas_call(
        _router_kernel, grid=(t // tt,),
        in_specs=[
            pl.BlockSpec((tt, d), lambda i: (i, 0)),
            pl.BlockSpec((ne, d), lambda i: (0, 0)),
            pl.BlockSpec((ne, 1), lambda i: (0, 0)),
            pl.BlockSpec((tt, tt), lambda i: (0, 0)),
            pl.BlockSpec((ne, ne), lambda i: (0, 0)),
        ],
        out_specs=[slot_spec, slot_spec, slot_spec, pl.BlockSpec((ne, 128), lambda i: (0, 0))],
        out_shape=[jax.ShapeDtypeStruct((TOP_K, t), jnp.int32), jax.ShapeDtypeStruct((TOP_K, t), jnp.int32),
                   jax.ShapeDtypeStruct((TOP_K, t), jnp.float32), jax.ShapeDtypeStruct((ne, 128), jnp.float32)],
        scratch_shapes=[pltpu.VMEM((ne, 1), jnp.float32)],
        compiler_params=_cp(("arbitrary",)), name="moe_router",
    )(hn, _bf16(router_w.T), router_bias.reshape(ne, 1).astype(jnp.float32), ut, lt)


def _dest_kernel(start_ref, eid_ref, pos_ref, o_ref):
    eid = eid_ref[...]
    dest = pos_ref[...]
    for e in range(start_ref.shape[0]):
        dest = dest + jnp.where(eid == e, start_ref[e], 0)
    o_ref[...] = dest


def moe_dest(pad_start, eid, posk):
    return pl.pallas_call(
        _dest_kernel,
        grid_spec=pltpu.PrefetchScalarGridSpec(
            num_scalar_prefetch=1, grid=(1,),
            in_specs=[pl.BlockSpec(eid.shape, lambda i, s: (0, 0)), pl.BlockSpec(eid.shape, lambda i, s: (0, 0))],
            out_specs=pl.BlockSpec(eid.shape, lambda i, s: (0, 0))),
        out_shape=jax.ShapeDtypeStruct(eid.shape, jnp.int32),
        compiler_params=_cp(("arbitrary",)), name="moe_dest",
    )(pad_start, eid, posk)


def _dispatch_kernel(last_ref, dest_ref, src_ref, out_ref, zbuf, sem, zsem, *, tt):
    bm = zbuf.shape[0]

    @pl.when(pl.program_id(0) == 0)
    def _():
        zbuf[...] = jnp.zeros_like(zbuf)

        def zero_copy(e):
            return pltpu.make_async_copy(zbuf, out_ref.at[pl.ds(pl.multiple_of(last_ref[e], bm), bm)], zsem)

        def zstart(e, carry):
            @pl.when(last_ref[e] >= 0)
            def _():
                zero_copy(e).start()
            return carry

        def zwait(e, carry):
            @pl.when(last_ref[e] >= 0)
            def _():
                zero_copy(e).wait()
            return carry

        lax.fori_loop(0, last_ref.shape[0], zstart, 0)
        lax.fori_loop(0, last_ref.shape[0], zwait, 0)

    def body(t, carry):
        for k in range(TOP_K):
            pltpu.make_async_copy(src_ref.at[pl.ds(t, 1)], out_ref.at[pl.ds(dest_ref[k, t], 1)], sem).start()
        return carry

    lax.fori_loop(0, tt, body, 0)
    for k in range(TOP_K):
        pltpu.make_async_copy(src_ref, out_ref.at[pl.ds(0, tt)], sem).wait()


def moe_dispatch(zero_blocks, dest, hp, n_rows, bm, tt=256):
    t, w = hp.shape
    tt = min(tt, t)
    grid_spec = pltpu.PrefetchScalarGridSpec(
        num_scalar_prefetch=1, grid=(t // tt,),
        in_specs=[
            pl.BlockSpec((TOP_K, tt), lambda i, last: (0, i), memory_space=pltpu.SMEM),
            pl.BlockSpec((tt, w), lambda i, last: (i, 0)),
        ],
        out_specs=pl.BlockSpec(memory_space=pl.ANY),
        scratch_shapes=[pltpu.VMEM((bm, w), jnp.uint32), pltpu.SemaphoreType.DMA(()), pltpu.SemaphoreType.DMA(())],
    )
    return pl.pallas_call(
        functools.partial(_dispatch_kernel, tt=tt), grid_spec=grid_spec,
        out_shape=jax.ShapeDtypeStruct((n_rows, w), jnp.uint32),
        compiler_params=pltpu.CompilerParams(dimension_semantics=("arbitrary",), has_side_effects=True),
        name="moe_dispatch",
    )(zero_blocks, dest, hp)


def _expert_kernel(be_ref, nu_ref, x_ref, wg_ref, wu_ref, wd_ref, o_ref, wg_s, wu_s, wd_s):
    b = pl.program_id(0)
    e = be_ref[b]
    prev = be_ref[jnp.maximum(b - 1, 0)]
    half = x_ref.shape[1]

    @pl.when((b == 0) | (e != prev))
    def _():
        wg_s[0] = _bf16(wg_ref[0:half, :])
        wg_s[1] = _bf16(wg_ref[half:, :])
        wu_s[0] = _bf16(wu_ref[0:half, :])
        wu_s[1] = _bf16(wu_ref[half:, :])
        wd_s[...] = _bf16(wd_ref[...])

    @pl.when(b < nu_ref[0])
    def _():
        xp = x_ref[...]
        xa = _bf16(_unpack_hi(xp))
        xb = _bf16(_unpack_lo(xp))
        g = _dot(xa, wg_s[0]) + _dot(xb, wg_s[1])
        u = _dot(xa, wu_s[0]) + _dot(xb, wu_s[1])
        h = _bf16(g * _sigmoid(g) * u)
        y = _dot(h, wd_s[...])
        o_ref[...] = _pack_bf16_pair(y[:, :half], y[:, half:])

    @pl.when(b >= nu_ref[0])
    def _():
        o_ref[...] = jnp.zeros_like(o_ref)


def moe_experts(block_expert, n_used, xs, w_gate, w_up, w_down, layer, bm):
    n_rows, half = xs.shape
    _, ne, d, f = w_gate.shape
    nblk = n_rows // bm
    grid_spec = pltpu.PrefetchScalarGridSpec(
        num_scalar_prefetch=2, grid=(nblk,),
        in_specs=[
            pl.BlockSpec((bm, half), lambda b, be, nu: (jnp.minimum(b, nu[0] - 1), 0)),
            pl.BlockSpec((None, None, d, f), lambda b, be, nu: (layer, be[b], 0, 0)),
            pl.BlockSpec((None, None, d, f), lambda b, be, nu: (layer, be[b], 0, 0)),
            pl.BlockSpec((None, None, f, d), lambda b, be, nu: (layer, be[b], 0, 0)),
        ],
        out_specs=pl.BlockSpec((bm, half), lambda b, be, nu: (b, 0)),
        scratch_shapes=[pltpu.VMEM((2, half, f), jnp.bfloat16), pltpu.VMEM((2, half, f), jnp.bfloat16),
                        pltpu.VMEM((f, d), jnp.bfloat16)],
    )
    return pl.pallas_call(
        _expert_kernel, grid_spec=grid_spec,
        out_shape=jax.ShapeDtypeStruct((n_rows, half), jnp.uint32),
        compiler_params=_cp(("arbitrary",)), name="moe_experts",
    )(block_expert, n_used, xs, w_gate, w_up, w_down)


def _combine_kernel(dest_ref, nxt_ref, ys_ref, gk_ref, sh_ref, x_ref, ga_ref, o_ref, buf0, buf1, sems, *, tt, nt):
    i = pl.program_id(0)
    bufs = (buf0, buf1)
    half = buf0.shape[2]

    def row_copy(d_ref, s, k, t):
        return pltpu.make_async_copy(ys_ref.at[pl.ds(d_ref[k, t], 1)], bufs[s].at[k, pl.ds(t, 1)], sems.at[s])

    def wait_tile(s):
        for k in range(TOP_K):
            pltpu.make_async_copy(ys_ref.at[pl.ds(0, tt)], bufs[s].at[k], sems.at[s]).wait()

    @pl.when(i == 0)
    def _():
        def body(t, carry):
            for k in range(TOP_K):
                row_copy(dest_ref, 0, k, t).start()
            return carry

        lax.fori_loop(0, tt, body, 0)

    def step(cur, nxt):
        wait_tile(cur)
        for t in range(tt):
            for k in range(TOP_K):
                row_copy(nxt_ref, nxt, k, t).start()
        acc_a = jnp.zeros((tt, half), jnp.float32)
        acc_b = jnp.zeros((tt, half), jnp.float32)
        for k in range(TOP_K):
            p = bufs[cur][k]
            gk = gk_ref[:, k:k + 1]
            acc_a = acc_a + gk * _unpack_hi(p)
            acc_b = acc_b + gk * _unpack_lo(p)
        sh = sh_ref[...].astype(jnp.float32)
        o_ref[:, 0:half] = x_ref[:, 0:half] + ga_ref[:, 0:half] * (acc_a + sh[:, 0:half])
        o_ref[:, half:] = x_ref[:, half:] + ga_ref[:, half:] * (acc_b + sh[:, half:])

    @pl.when(i % 2 == 0)
    def _():
        step(0, 1)

    @pl.when(i % 2 == 1)
    def _():
        step(1, 0)

    @pl.when(i == nt - 1)
    def _():
        wait_tile(nt % 2)


def moe_combine(dest, ys, gates_tk, shared, x2d, gate_ada, seq, tt=128):
    t, d = x2d.shape
    half = ys.shape[1]
    tt = min(tt, seq)
    nt = t // tt
    return pl.pallas_call(
        functools.partial(_combine_kernel, tt=tt, nt=nt), grid=(nt,),
        in_specs=[
            pl.BlockSpec((TOP_K, tt), lambda i: (0, i), memory_space=pltpu.SMEM),
            pl.BlockSpec((TOP_K, tt), lambda i: (0, jnp.minimum(i + 1, nt - 1)), memory_space=pltpu.SMEM),
            pl.BlockSpec(memory_space=pl.ANY),
            pl.BlockSpec((tt, TOP_K), lambda i: (i, 0)),
            pl.BlockSpec((tt, d), lambda i: (i, 0)),
            pl.BlockSpec((tt, d), lambda i: (i, 0)),
            pl.BlockSpec((None, 1, d), lambda i: ((i * tt) // seq, 0, 0)),
        ],
        out_specs=pl.BlockSpec((tt, d), lambda i: (i, 0)),
        out_shape=jax.ShapeDtypeStruct((t, d), jnp.float32),
        scratch_shapes=[pltpu.VMEM((TOP_K, tt, half), jnp.uint32), pltpu.VMEM((TOP_K, tt, half), jnp.uint32),
                        pltpu.SemaphoreType.DMA((2,))],
        compiler_params=_cp(("arbitrary",)), name="moe_combine",
    )(dest, dest, ys, gates_tk, shared, x2d, gate_ada)


MOE_ROWS = 256


def moe_layer(x2d, hn, hp, gate_ada, seq, router_w, router_bias, w_gate, w_up, w_down, layer, s_gate, s_up, s_down):
    t, d = x2d.shape
    ne = router_w.shape[1]
    bm = min(MOE_ROWS, t)
    eid, posk, gates_k, cnt = moe_router(hn, router_w, router_bias)
    counts = cnt[:, 0].astype(jnp.int32)
    padded = (counts + bm - 1) // bm * bm
    pad_end = jnp.cumsum(padded)
    pad_start = pad_end - padded
    n_rows = t * TOP_K + ne * bm
    nblk = n_rows // bm
    dest = moe_dest(pad_start.astype(jnp.int32), eid, posk)
    block_row = jnp.arange(nblk, dtype=jnp.int32) * bm
    block_expert = jnp.minimum(jnp.sum(pad_end[None, :] <= block_row[:, None], axis=1), ne - 1).astype(jnp.int32)
    n_used = (pad_end[-1:] // bm).astype(jnp.int32)
    tail = (n_used[0] + jnp.arange(ne, dtype=jnp.int32)) * bm
    zero_blocks = jnp.concatenate([jnp.where(padded > counts, pad_end - bm, -1),
                                   jnp.where(tail < n_rows, tail, -1)]).astype(jnp.int32)
    xs = moe_dispatch(zero_blocks, dest, hp, n_rows, bm)
    ys = moe_experts(block_expert, n_used, xs, w_gate, w_up, w_down, layer, bm)
    hs = matmul_swiglu(hn, _bf16(s_gate), _bf16(s_up))
    shared = matmul(hs, _bf16(s_down), jnp.bfloat16)
    return moe_combine(dest, ys, gates_k.T, shared, x2d, gate_ada, seq)


def kernel(x, c, mix_norm_g, mix_ada_w, mix_ada_b, ev_w_in, ev_conv_w, ev_conv_b, ev_ln_g, ev_ln_b, ev_q_norm_g, ev_k_norm_g, ev_w_out, od_w_in, od_log_dt, od_lambda_re, od_lambda_im, od_b_re, od_b_im, od_c_re, od_c_im, od_d, od_w_out, ffn_norm_g, ffn_ada_w, ffn_ada_b, router_w, router_bias, exp_gate, exp_up, exp_down, sh_gate, sh_up, sh_down):
    bsz, seq, d = x.shape
    depth = mix_norm_g.shape[0]
    t = bsz * seq
    c_pad = jnp.zeros((8, d), jnp.float32).at[:bsz].set(c)
    mix_mod = adaln_all(c_pad, mix_ada_w, mix_ada_b)[:, :bsz]
    ffn_mod = adaln_all(c_pad, ffn_ada_w, ffn_ada_b)[:, :bsz]

    def split(mod):
        return tuple(mod[:, None, k * d:(k + 1) * d] for k in range(3))

    for i in range(depth):
        j = i // 2
        shift, scale, gate = split(mix_mod[i])
        hn = norm_modulate(x, mix_norm_g[i], scale, shift)
        x2d = x.reshape(t, d)
        if i % 2 == 0:
            cc = ev_conv_w.shape[-1]
            dh = ev_q_norm_g.shape[-1]
            sbw = (ev_w_in.shape[-1] - 2 * cc) // 3
            proj = matmul(hn, _bf16(ev_w_in[j]), jnp.bfloat16)
            a = conv_branch(proj, ev_conv_w[j], ev_conv_b[j], ev_ln_g[j], ev_ln_b[j], seq)
            qn, kn = qk_norm(proj, ev_q_norm_g[j], ev_k_norm_g[j], 2 * cc, sbw)
            o = stickbreak_attention(qn, kn, proj, 2 * cc + 2 * sbw, bsz, seq, dh)
            x2d = matmul2_resid(a, o, _bf16(ev_w_out[j]), x2d, gate, seq)
        else:
            u = matmul(hn, _bf16(od_w_in[j]), jnp.float32)
            wts = s5_weights(od_log_dt[j], od_lambda_re[j], od_lambda_im[j], od_b_re[j], od_b_im[j],
                             od_c_re[j], od_c_im[j], od_d[j])
            y = s5_scan(u.reshape(bsz, seq, -1), *wts)
            x2d = matmul_glu_resid(y.reshape(t, -1), _bf16(od_w_out[j]), x2d, gate, seq)
        x = x2d.reshape(bsz, seq, d)
        shift, scale, gate = split(ffn_mod[i])
        hn, hp = norm_modulate(x, ffn_norm_g[i], scale, shift, packed=True)
        x2d = moe_layer(x.reshape(t, d), hn, hp, gate, seq, router_w[i], router_bias[i], exp_gate, exp_up,
                        exp_down, i, sh_gate[i], sh_up[i], sh_down[i])
        x = x2d.reshape(bsz, seq, d)
    return x
```

```python
import functools
import math

import jax
import jax.numpy as jnp
from jax import lax
from jax.experimental import pallas as pl
from jax.experimental.pallas import tpu as pltpu

EPS = 1e-6
TOP_K = 8
N_GROUPS = 8
TOPK_GROUPS = 4
ROUTED_SCALE = 2.5
S5_CHUNK = 8
S5_BLOCK_LANES = 128

V7X_VMEM_LIMIT = 56 * 1024 * 1024


def _cp(dims, vmem=V7X_VMEM_LIMIT):
    return pltpu.CompilerParams(dimension_semantics=dims, vmem_limit_bytes=vmem)


def _sigmoid(x):
    return 1.0 / (1.0 + jnp.exp(-x))


def _bf16(x):
    return x.astype(jnp.bfloat16)


def _dot(a, b):
    return jnp.dot(a, b, preferred_element_type=jnp.float32)


def _tile(n, target, quantum=128):
    if n <= target:
        return n
    best = quantum
    for cand in range(quantum, target + 1, quantum):
        if n % cand == 0:
            best = cand
    assert n % best == 0, (n, target)
    return best


def _dot_nt(a, b):
    return lax.dot_general(a, b, (((1,), (1,)), ((), ())), preferred_element_type=jnp.float32)


def _adaln_kernel(c_ref, w_ref, b_ref, o_ref):
    c = c_ref[...]
    sc = _bf16(c * _sigmoid(c))
    o_ref[...] = _dot(sc, _bf16(w_ref[...])) + b_ref[...]


def adaln_all(c_pad, w, b, tn=512):
    nl, d, n = w.shape
    tn = min(tn, n)
    return pl.pallas_call(
        _adaln_kernel,
        grid=(nl, n // tn),
        in_specs=[
            pl.BlockSpec((8, d), lambda l, j: (0, 0)),
            pl.BlockSpec((None, d, tn), lambda l, j: (l, 0, j)),
            pl.BlockSpec((None, 1, tn), lambda l, j: (l, 0, j)),
        ],
        out_specs=pl.BlockSpec((None, 8, tn), lambda l, j: (l, 0, j)),
        out_shape=jax.ShapeDtypeStruct((nl, 8, n), jnp.float32),
        compiler_params=_cp(("parallel", "parallel")),
        name="adaln",
    )(c_pad, w, b.reshape(nl, 1, n))


def _pack_bf16_pair(a, b):
    ua = pltpu.bitcast(_bf16(a).astype(jnp.float32), jnp.uint32)
    ub = pltpu.bitcast(_bf16(b).astype(jnp.float32), jnp.uint32)
    return (ua & jnp.uint32(0xFFFF0000)) | (ub >> 16)


def _unpack_hi(p):
    return pltpu.bitcast(p & jnp.uint32(0xFFFF0000), jnp.float32)


def _unpack_lo(p):
    return pltpu.bitcast(p << 16, jnp.float32)


def _norm_mod(x, g_ref, sc_ref, sh_ref):
    xf = x.astype(jnp.float32)
    ms = jnp.mean(xf * xf, axis=-1, keepdims=True)
    return xf * lax.rsqrt(ms + EPS) * g_ref[...] * (1.0 + sc_ref[...]) + sh_ref[...]


def _norm_kernel(x_ref, g_ref, sc_ref, sh_ref, o_ref):
    o_ref[...] = _bf16(_norm_mod(x_ref[...], g_ref, sc_ref, sh_ref))


def _norm_pack_kernel(x_ref, g_ref, sc_ref, sh_ref, o_ref, p_ref):
    y = _norm_mod(x_ref[...], g_ref, sc_ref, sh_ref)
    o_ref[...] = _bf16(y)
    h = y.shape[-1] // 2
    p_ref[...] = _pack_bf16_pair(y[:, :h], y[:, h:])


def norm_modulate(x, g, scale, shift, packed=False, ts=256):
    b, s, d = x.shape
    ts = min(ts, s)
    nt = s // ts
    in_specs = [
        pl.BlockSpec((None, ts, d), lambda i, j: (i, j, 0)),
        pl.BlockSpec((1, d), lambda i, j: (0, 0)),
        pl.BlockSpec((None, 1, d), lambda i, j: (i, 0, 0)),
        pl.BlockSpec((None, 1, d), lambda i, j: (i, 0, 0)),
    ]
    o_spec = pl.BlockSpec((ts, d), lambda i, j: (i * nt + j, 0))
    o_shape = jax.ShapeDtypeStruct((b * s, d), jnp.bfloat16)
    if not packed:
        return pl.pallas_call(
            _norm_kernel, grid=(b, nt), in_specs=in_specs, out_specs=o_spec, out_shape=o_shape,
            compiler_params=_cp(("parallel", "parallel")), name="norm_mod",
        )(x, g.reshape(1, d), scale, shift)
    return pl.pallas_call(
        _norm_pack_kernel, grid=(b, nt), in_specs=in_specs,
        out_specs=[o_spec, pl.BlockSpec((ts, d // 2), lambda i, j: (i * nt + j, 0))],
        out_shape=[o_shape, jax.ShapeDtypeStruct((b * s, d // 2), jnp.uint32)],
        compiler_params=_cp(("parallel", "parallel")), name="norm_mod_pack",
    )(x, g.reshape(1, d), scale, shift)


def _mm_kernel(a_ref, w_ref, o_ref):
    o_ref[...] = _dot(a_ref[...], w_ref[...]).astype(o_ref.dtype)


def matmul(a, w, out_dtype, bm=1024, bn=1024):
    m, k = a.shape
    n = w.shape[1]
    bm, bn = _tile(m, bm), _tile(n, bn)
    return pl.pallas_call(
        _mm_kernel, grid=(m // bm, n // bn),
        in_specs=[pl.BlockSpec((bm, k), lambda i, j: (i, 0)), pl.BlockSpec((k, bn), lambda i, j: (0, j))],
        out_specs=pl.BlockSpec((bm, bn), lambda i, j: (i, j)),
        out_shape=jax.ShapeDtypeStruct((m, n), out_dtype),
        compiler_params=_cp(("parallel", "parallel")), name="matmul",
    )(a, w)


def _mm2_resid_kernel(a1_ref, a2_ref, w1_ref, w2_ref, x_ref, g_ref, o_ref):
    m = _dot(a1_ref[...], w1_ref[...]) + _dot(a2_ref[...], w2_ref[...])
    o_ref[...] = x_ref[...] + g_ref[...] * m


def matmul2_resid(a1, a2, w, x2d, gate, seq, bm=1024, bn=1024):
    m, k1 = a1.shape
    k2 = a2.shape[1]
    assert k1 == k2
    n = w.shape[1]
    bm, bn = _tile(seq, bm), _tile(n, bn)
    return pl.pallas_call(
        _mm2_resid_kernel, grid=(m // bm, n // bn),
        in_specs=[
            pl.BlockSpec((bm, k1), lambda i, j: (i, 0)),
            pl.BlockSpec((bm, k2), lambda i, j: (i, 0)),
            pl.BlockSpec((k1, bn), lambda i, j: (0, j)),
            pl.BlockSpec((k2, bn), lambda i, j: (1, j)),
            pl.BlockSpec((bm, bn), lambda i, j: (i, j)),
            pl.BlockSpec((None, 1, bn), lambda i, j: ((i * bm) // seq, 0, j)),
        ],
        out_specs=pl.BlockSpec((bm, bn), lambda i, j: (i, j)),
        out_shape=jax.ShapeDtypeStruct((m, n), jnp.float32),
        compiler_params=_cp(("parallel", "parallel")), name="matmul2_resid",
    )(a1, a2, w, w, x2d, gate)


def _mm_glu_resid_kernel(a_ref, wa_ref, wb_ref, x_ref, g_ref, o_ref):
    a = a_ref[...]
    va = _dot(a, wa_ref[...])
    vb = _dot(a, wb_ref[...])
    o_ref[...] = x_ref[...] + g_ref[...] * (va * _sigmoid(vb))


def matmul_glu_resid(a, w, x2d, gate, seq, bm=1024, bn=512):
    m, k = a.shape
    n = w.shape[1] // 2
    bm, bn = _tile(seq, bm), _tile(n, bn)
    nb = n // bn
    return pl.pallas_call(
        _mm_glu_resid_kernel, grid=(m // bm, nb),
        in_specs=[
            pl.BlockSpec((bm, k), lambda i, j: (i, 0)),
            pl.BlockSpec((k, bn), lambda i, j: (0, j)),
            pl.BlockSpec((k, bn), lambda i, j: (0, j + nb)),
            pl.BlockSpec((bm, bn), lambda i, j: (i, j)),
            pl.BlockSpec((None, 1, bn), lambda i, j: ((i * bm) // seq, 0, j)),
        ],
        out_specs=pl.BlockSpec((bm, bn), lambda i, j: (i, j)),
        out_shape=jax.ShapeDtypeStruct((m, n), jnp.float32),
        compiler_params=_cp(("parallel", "parallel")), name="matmul_glu_resid",
    )(a, w, w, x2d, gate)


CONV_HALO = 32
CONV_ROWS = 64
CONV_LANES = 256


def _conv_kernel(v_ref, g_ref, vh_ref, gh_ref, w_ref, cb_ref, lg_ref, lb_ref, o_ref, abuf, cbuf, *,
                 tiles_per_seq, width):
    ts, cc = o_ref.shape
    first = (pl.program_id(0) % tiles_per_seq) == 0
    vh = vh_ref[...].astype(jnp.float32)
    gh = gh_ref[...].astype(jnp.float32)
    abuf[0:CONV_HALO, :] = jnp.where(first, 0.0, vh * _sigmoid(gh))
    v = v_ref[...].astype(jnp.float32)
    g = g_ref[...].astype(jnp.float32)
    abuf[CONV_HALO:, :] = v * _sigmoid(g)
    off = CONV_HALO - (width - 1)
    for lc in range(cc // CONV_LANES):
        ls = slice(lc * CONV_LANES, (lc + 1) * CONV_LANES)
        for rc in range(ts // CONV_ROWS):
            r0 = rc * CONV_ROWS
            acc = jnp.zeros((CONV_ROWS, CONV_LANES), jnp.float32)
            for j in range(width):
                acc = acc + w_ref[j:j + 1, ls] * abuf[r0 + off + j:r0 + off + j + CONV_ROWS, ls]
            cbuf[r0:r0 + CONV_ROWS, ls] = acc + cb_ref[:, ls]
    c = cbuf[...]
    mu = jnp.mean(c, axis=-1, keepdims=True)
    cen = c - mu
    var = jnp.mean(cen * cen, axis=-1, keepdims=True)
    y = cen * lax.rsqrt(var + EPS) * lg_ref[...] + lb_ref[...]
    o_ref[...] = _bf16(y * _sigmoid(y))


def conv_branch(proj, conv_w, conv_b, ln_g, ln_b, seq, ts=128):
    t = proj.shape[0]
    width, cc = conv_w.shape
    ts = min(ts, seq)
    assert width - 1 <= CONV_HALO and ts % CONV_HALO == 0 and cc % CONV_LANES == 0 and ts % CONV_ROWS == 0
    r = ts // CONV_HALO
    halo_idx = lambda i: jnp.maximum(i * r - 1, 0)
    kern = functools.partial(_conv_kernel, tiles_per_seq=seq // ts, width=width)
    return pl.pallas_call(
        kern, grid=(t // ts,),
        in_specs=[
            pl.BlockSpec((ts, cc), lambda i: (i, 0)),
            pl.BlockSpec((ts, cc), lambda i: (i, 1)),
            pl.BlockSpec((CONV_HALO, cc), lambda i: (halo_idx(i), 0)),
            pl.BlockSpec((CONV_HALO, cc), lambda i: (halo_idx(i), 1)),
            pl.BlockSpec((width, cc), lambda i: (0, 0)),
            pl.BlockSpec((1, cc), lambda i: (0, 0)),
            pl.BlockSpec((1, cc), lambda i: (0, 0)),
            pl.BlockSpec((1, cc), lambda i: (0, 0)),
        ],
        out_specs=pl.BlockSpec((ts, cc), lambda i: (i, 0)),
        out_shape=jax.ShapeDtypeStruct((t, cc), jnp.bfloat16),
        scratch_shapes=[pltpu.VMEM((CONV_HALO + ts, cc), jnp.float32), pltpu.VMEM((ts, cc), jnp.float32)],
        compiler_params=_cp(("parallel",)), name="conv_branch",
    )(proj, proj, proj, proj, conv_w, conv_b.reshape(1, cc), ln_g.reshape(1, cc), ln_b.reshape(1, cc))


def _qknorm_kernel(q_ref, k_ref, gq_ref, gk_ref, qo_ref, ko_ref, *, dh):
    q_scale = math.log2(math.e) / math.sqrt(dh)
    for src, g_ref, dst, mul in ((q_ref, gq_ref, qo_ref, q_scale), (k_ref, gk_ref, ko_ref, 1.0)):
        for h in range(src.shape[1] // dh):
            ls = slice(h * dh, (h + 1) * dh)
            xf = src[:, ls].astype(jnp.float32)
            ms = jnp.mean(xf * xf, axis=-1, keepdims=True)
            dst[:, ls] = _bf16(xf * lax.rsqrt(ms + EPS) * g_ref[...] * mul)


def qk_norm(proj, gq, gk, col0, width, ts=512):
    t = proj.shape[0]
    dh = gq.shape[0]
    ts = min(ts, t)
    qb, kb = col0 // width, col0 // width + 1
    return pl.pallas_call(
        functools.partial(_qknorm_kernel, dh=dh), grid=(t // ts,),
        in_specs=[
            pl.BlockSpec((ts, width), lambda i: (i, qb)),
            pl.BlockSpec((ts, width), lambda i: (i, kb)),
            pl.BlockSpec((1, dh), lambda i: (0, 0)),
            pl.BlockSpec((1, dh), lambda i: (0, 0)),
        ],
        out_specs=[pl.BlockSpec((ts, width), lambda i: (i, 0)), pl.BlockSpec((ts, width), lambda i: (i, 0))],
        out_shape=[jax.ShapeDtypeStruct((t, width), jnp.bfloat16)] * 2,
        compiler_params=_cp(("parallel",)), name="qk_norm",
    )(proj, proj, gq.reshape(1, dh), gk.reshape(1, dh))


ATT_TILE = 1024
ATT_SUB = 256


def _attn_kernel(qi_ref, kj_ref, q_ref, k_ref, v_ref, u_ref, m_ref, o_ref, acc_ref, r_ref):
    p = pl.program_id(2)
    qi = qi_ref[p]
    kj = kj_ref[p]
    tq = q_ref.shape[0]
    tk = k_ref.shape[0]
    sub = u_ref.shape[1]

    @pl.when(kj == qi)
    def _():
        acc_ref[...] = jnp.zeros_like(acc_ref)
        r_ref[...] = jnp.zeros_like(r_ref)

    def process(diagonal):
        nsb = tk // sub
        z = _dot_nt(q_ref[...], k_ref[...])
        zb = _bf16(z)
        one = jnp.ones((), jnp.bfloat16)
        log2e = jnp.asarray(math.log2(math.e), jnp.bfloat16)
        lk = -(jnp.maximum(zb, 0) + jnp.log(one + jnp.exp2(-jnp.abs(zb))) * log2e)
        lm = lk * m_ref[...] if diagonal else lk
        r = r_ref[...]
        acc = acc_ref[...]
        for s in reversed(range(nsb)):
            ks = slice(s * sub, (s + 1) * sub)
            cs = _dot(lm[:, ks], u_ref[...])
            a = _bf16(jnp.exp2(z[:, ks] + lk[:, ks].astype(jnp.float32) + (cs + r)))
            if diagonal:
                a = a * m_ref[:, ks]
            acc = acc + _dot(a, v_ref[ks, :])
            r = r + cs[:, 0:1] + lm[:, s * sub:s * sub + 1].astype(jnp.float32)
        r_ref[...] = r
        acc_ref[...] = acc

    @pl.when(kj == qi)
    def _():
        process(True)

    @pl.when(kj != qi)
    def _():
        process(False)

    @pl.when(kj == 0)
    def _():
        o_ref[...] = acc_ref[...].astype(o_ref.dtype)


def stickbreak_attention(qn, kn, proj, vcol0, batch, seq, dh):
    t, w = qn.shape
    heads = w // dh
    tile = min(ATT_TILE, seq)
    sub = min(ATT_SUB, tile)
    nq = seq // tile
    pairs = [(i, j) for i in range(nq) for j in range(i, -1, -1)]
    qi = jnp.asarray([p[0] for p in pairs], jnp.int32)
    kj = jnp.asarray([p[1] for p in pairs], jnp.int32)
    r = lax.broadcasted_iota(jnp.int32, (sub, sub), 0)
    c = lax.broadcasted_iota(jnp.int32, (sub, sub), 1)
    tri = _bf16((r > c).astype(jnp.float32))
    rt = lax.broadcasted_iota(jnp.int32, (tile, tile), 0)
    ct = lax.broadcasted_iota(jnp.int32, (tile, tile), 1)
    causal = _bf16((ct < rt).astype(jnp.float32))
    vb = vcol0 // dh
    grid_spec = pltpu.PrefetchScalarGridSpec(
        num_scalar_prefetch=2,
        grid=(batch, heads, len(pairs)),
        in_specs=[
            pl.BlockSpec((tile, dh), lambda b, h, p, qi, kj: (b * nq + qi[p], h)),
            pl.BlockSpec((tile, dh), lambda b, h, p, qi, kj: (b * nq + kj[p], h)),
            pl.BlockSpec((tile, dh), lambda b, h, p, qi, kj: (b * nq + kj[p], vb + h)),
            pl.BlockSpec((sub, sub), lambda b, h, p, qi, kj: (0, 0)),
            pl.BlockSpec((tile, tile), lambda b, h, p, qi, kj: (0, 0)),
        ],
        out_specs=pl.BlockSpec((tile, dh), lambda b, h, p, qi, kj: (b * nq + qi[p], h)),
        scratch_shapes=[pltpu.VMEM((tile, dh), jnp.float32), pltpu.VMEM((tile, 1), jnp.float32)],
    )
    return pl.pallas_call(
        _attn_kernel, grid_spec=grid_spec,
        out_shape=jax.ShapeDtypeStruct((t, w), jnp.bfloat16),
        compiler_params=_cp(("parallel", "parallel", "arbitrary")), name="stickbreak_attn",
    )(qi, kj, qn, kn, proj, tri, causal)


def _gelu_tanh(x):
    return 0.5 * x * (1.0 + jnp.tanh(math.sqrt(2.0 / math.pi) * (x + 0.044715 * (x * x * x))))


def _s5_kernel(u_ref, wm_ref, we_ref, wc_ref, lam_ref, y_ref, s_ref, e_ref, sp_ref, yb_ref):
    tt, bl = u_ref.shape
    nch = tt // S5_CHUNK
    half = s_ref.shape[1]

    @pl.when(pl.program_id(2) == 0)
    def _():
        s_ref[...] = jnp.zeros_like(s_ref)

    ucat = _bf16(jnp.concatenate(
        [u_ref[pl.ds(j, nch, stride=S5_CHUNK), :] for j in range(S5_CHUNK)], axis=1))
    e_ref[...] = _dot(ucat, we_ref[...])
    lam_re = lam_ref[0:1, :]
    lam_im = lam_ref[1:2, :]

    def body(c, carry):
        sre, sim = carry
        sp_ref[pl.ds(c, 1), 0:half] = sre
        sp_ref[pl.ds(c, 1), half:] = sim
        ere = e_ref[pl.ds(c, 1), 0:half]
        eim = e_ref[pl.ds(c, 1), half:]
        return (lam_re * sre - lam_im * sim + ere, lam_re * sim + lam_im * sre + eim)

    sre, sim = lax.fori_loop(0, nch, body, (s_ref[0:1, :], s_ref[1:2, :]), unroll=8)
    s_ref[0:1, :] = sre
    s_ref[1:2, :] = sim
    ycat = _dot(ucat, wm_ref[...]) + _dot(_bf16(sp_ref[...]), wc_ref[...])
    for j in range(S5_CHUNK):
        yb_ref[pl.ds(j, nch, stride=S5_CHUNK), :] = _gelu_tanh(ycat[:, j * bl:(j + 1) * bl])
    y_ref[...] = _bf16(yb_ref[...])


def _s5_expand_kernel(c_ref, e_ref, m_ref, o_ref):
    o_ref[...] = _bf16(_dot(c_ref[...], e_ref[...])) * m_ref[...]


def s5_expand(compact, spread, same_group):
    nb, rows, cc = compact.shape
    cols = spread.shape[1]
    return pl.pallas_call(
        _s5_expand_kernel, grid=(nb,),
        in_specs=[pl.BlockSpec((None, rows, cc), lambda b: (b, 0, 0)),
                  pl.BlockSpec((cc, cols), lambda b: (0, 0)),
                  pl.BlockSpec((rows, cols), lambda b: (0, 0))],
        out_specs=pl.BlockSpec((None, rows, cols), lambda b: (b, 0, 0)),
        out_shape=jax.ShapeDtypeStruct((nb, rows, cols), jnp.bfloat16),
        compiler_params=_cp(("parallel",)), name="s5_expand",
    )(compact, spread, same_group)


def s5_weights(log_dt, lam_re, lam_im, b_re, b_im, c_re, c_im, d_skip):
    g, p = lam_re.shape
    hg = b_re.shape[-1]
    L = S5_CHUNK
    gb = S5_BLOCK_LANES // hg
    nb = g // gb
    dt = jnp.exp(log_dt)[:, None]
    mag = jnp.exp(lam_re * dt)
    ang = lam_im * dt
    lbar_re = mag * jnp.cos(ang)
    lbar_im = mag * jnp.sin(ang)
    den = lam_re * lam_re + lam_im * lam_im
    nr = lbar_re - 1.0
    f_re = (nr * lam_re + lbar_im * lam_im) / den
    f_im = (lbar_im * lam_re - nr * lam_im) / den
    bbar_re = f_re[..., None] * b_re - f_im[..., None] * b_im
    bbar_im = f_re[..., None] * b_im + f_im[..., None] * b_re
    n = jnp.arange(L + 1, dtype=jnp.float32)[:, None, None]
    pw_mag = jnp.exp(n * (lam_re * dt)[None])
    pw_re = pw_mag * jnp.cos(n * ang[None])
    pw_im = pw_mag * jnp.sin(n * ang[None])
    lb_re = pw_re[..., None] * bbar_re[None] - pw_im[..., None] * bbar_im[None]
    lb_im = pw_re[..., None] * bbar_im[None] + pw_im[..., None] * bbar_re[None]
    ktau = (jnp.einsum('gop,ngpi->ngoi', c_re, lb_re[:L], precision='highest')
            - jnp.einsum('gop,ngpi->ngoi', c_im, lb_im[:L], precision='highest'))
    jj = jnp.arange(L)[:, None]
    tt = jnp.arange(L)[None, :]
    tau = jnp.clip(tt - jj, 0, L - 1)
    kjt = jnp.where((tt >= jj)[..., None, None, None], ktau[tau], 0.0)
    kjt = kjt + (jj == tt)[..., None, None, None] * (d_skip[:, :, None] * jnp.eye(hg))[None, None]

    def block_diag(small):
        nb_, r0, g_, r1, c0, c1 = small.shape
        rows, cc, cols = r0 * g_ * r1, c0 * c1, c0 * g_ * c1
        ri = jnp.arange(rows)[:, None]
        ki = jnp.arange(cc)[:, None]
        ci = jnp.arange(cols)[None, :]
        spread = (ki // c1 == ci // (g_ * c1)) & (ki % c1 == ci % c1)
        same_group = (ri // r1) % g_ == (ci // c1) % g_
        return s5_expand(_bf16(small.reshape(nb_, rows, cc)), _bf16(spread.astype(jnp.float32)),
                         _bf16(same_group.astype(jnp.float32)))

    w_main = block_diag(kjt.reshape(L, L, nb, gb, hg, hg).transpose(2, 0, 3, 5, 1, 4))
    le_re = lb_re[L - 1 - jnp.arange(L)].reshape(L, nb, gb, p, hg)
    le_im = lb_im[L - 1 - jnp.arange(L)].reshape(L, nb, gb, p, hg)
    le = jnp.stack([le_re, le_im], axis=0)
    w_end = block_diag(le.transpose(2, 1, 3, 5, 0, 4))
    m_re = c_re[None] * pw_re[1:, :, None, :] - c_im[None] * pw_im[1:, :, None, :]
    m_im = c_re[None] * pw_im[1:, :, None, :] + c_im[None] * pw_re[1:, :, None, :]
    mc = jnp.stack([m_re, -m_im], axis=0).reshape(2, L, nb, gb, hg, p)
    w_carry = block_diag(mc.transpose(2, 0, 3, 5, 1, 4))
    lam_l = jnp.stack([pw_re[L].reshape(nb, gb * p), pw_im[L].reshape(nb, gb * p)], axis=1)
    return w_main, w_end, w_carry, lam_l


def s5_scan(u, w_main, w_end, w_carry, lam_l, tt=2048):
    b, s, w = u.shape
    nb, kdim, sdim = w_end.shape
    bl = S5_BLOCK_LANES
    tt = min(tt, s)
    nch = tt // S5_CHUNK
    return pl.pallas_call(
        _s5_kernel, grid=(b, nb, s // tt),
        in_specs=[
            pl.BlockSpec((None, tt, bl), lambda i, j, t: (i, t, j)),
            pl.BlockSpec((None, kdim, kdim), lambda i, j, t: (j, 0, 0)),
            pl.BlockSpec((None, kdim, sdim), lambda i, j, t: (j, 0, 0)),
            pl.BlockSpec((None, sdim, kdim), lambda i, j, t: (j, 0, 0)),
            pl.BlockSpec((None, 2, sdim // 2), lambda i, j, t: (j, 0, 0)),
        ],
        out_specs=pl.BlockSpec((None, tt, bl), lambda i, j, t: (i, t, j)),
        out_shape=jax.ShapeDtypeStruct((b, s, w), jnp.bfloat16),
        scratch_shapes=[pltpu.VMEM((2, sdim // 2), jnp.float32), pltpu.VMEM((nch, sdim), jnp.float32),
                        pltpu.VMEM((nch, sdim), jnp.float32), pltpu.VMEM((tt, bl), jnp.float32)],
        compiler_params=_cp(("parallel", "parallel", "arbitrary")), name="s5_scan",
    )(u, w_main, w_end, w_carry, lam_l)


def _router_kernel(h_ref, rw_ref, bias_ref, ut_ref, lt_ref, eid_ref, posk_ref, gatek_ref, cnt_ref, run_ref):
    ne = rw_ref.shape[0]
    tt = h_ref.shape[0]
    per = ne // N_GROUPS

    @pl.when(pl.program_id(0) == 0)
    def _():
        run_ref[...] = jnp.zeros_like(run_ref)

    s = _sigmoid(_dot_nt(rw_ref[...], h_ref[...]))
    bsc = s + bias_ref[...]
    b3 = bsc.reshape(N_GROUPS, per, tt)
    m1 = jnp.max(b3, axis=1, keepdims=True)
    is1 = b3 == m1
    n1 = jnp.sum(is1.astype(jnp.float32), axis=1, keepdims=True)
    m2 = jnp.max(jnp.where(is1, -jnp.inf, b3), axis=1, keepdims=True)
    gs = (m1 + jnp.where(n1 >= 2.0, m1, m2)).reshape(N_GROUPS, tt)
    gidx = lax.broadcasted_iota(jnp.int32, (N_GROUPS, tt), 0)
    grank = jnp.zeros((N_GROUPS, tt), jnp.float32)
    for g in range(N_GROUPS):
        row = gs[g:g + 1, :]
        ahead = (row > gs) | ((row == gs) & (g < gidx))
        grank = grank + ahead.astype(jnp.float32)
    gok = (grank < float(TOPK_GROUPS)).astype(jnp.float32)
    eok = jnp.broadcast_to(gok.reshape(N_GROUPS, 1, tt), (N_GROUPS, per, tt)).reshape(ne, tt)
    masked = jnp.where(eok > 0.0, bsc, -jnp.inf)
    eidx = lax.broadcasted_iota(jnp.int32, (ne, tt), 0)
    erank = jnp.zeros((ne, tt), jnp.float32)
    for e in range(ne):
        row = masked[e:e + 1, :]
        ahead = (row > masked) | ((row == masked) & (e < eidx))
        erank = erank + ahead.astype(jnp.float32)
    sel = erank < float(TOP_K)
    self32 = sel.astype(jnp.float32)
    selb = _bf16(self32)
    gsel = jnp.where(sel, s, 0.0)
    gsum = jnp.sum(gsel, axis=0, keepdims=True)
    gate = gsel / gsum * ROUTED_SCALE
    pos = _dot(selb, ut_ref[...]) + run_ref[...]
    run_ref[...] += jnp.sum(self32, axis=1, keepdims=True)
    cnt_ref[...] = jnp.broadcast_to(run_ref[...], cnt_ref.shape)
    slot = _dot(lt_ref[...], selb)
    eidf = eidx.astype(jnp.float32)
    for k in range(TOP_K):
        m = sel & (slot == float(k))
        eid_ref[k:k + 1, :] = jnp.sum(jnp.where(m, eidf, 0.0), axis=0, keepdims=True).astype(jnp.int32)
        posk_ref[k:k + 1, :] = jnp.sum(jnp.where(m, pos, 0.0), axis=0, keepdims=True).astype(jnp.int32)
        gatek_ref[k:k + 1, :] = jnp.sum(jnp.where(m, gate, 0.0), axis=0, keepdims=True)


def moe_router(hn, router_w, router_bias, tt=512):
    t, d = hn.shape
    ne = router_w.shape[1]
    tt = min(tt, t)
    r = lax.broadcasted_iota(jnp.int32, (tt, tt), 0)
    c = lax.broadcasted_iota(jnp.int32, (tt, tt), 1)
    ut = _bf16((r < c).astype(jnp.float32))
    re = lax.broadcasted_iota(jnp.int32, (ne, ne), 0)
    ce = lax.broadcasted_iota(jnp.int32, (ne, ne), 1)
    lt = _bf16((ce < re).astype(jnp.float32))
    slot_spec = pl.BlockSpec((TOP_K, tt), lambda i: (0, i))
    return pl.pallas_call(
        _router_kernel, grid=(t // tt,),
        in_specs=[
            pl.BlockSpec((tt, d), lambda i: (i, 0)),
            pl.BlockSpec((ne, d), lambda i: (0, 0)),
            pl.BlockSpec((ne, 1), lambda i: (0, 0)),
            pl.BlockSpec((tt, tt), lambda i: (0, 0)),
            pl.BlockSpec((ne, ne), lambda i: (0, 0)),
        ],
        out_specs=[slot_spec, slot_spec, slot_spec, pl.BlockSpec((ne, 128), lambda i: (0, 0))],
        out_shape=[jax.ShapeDtypeStruct((TOP_K, t), jnp.int32), jax.ShapeDtypeStruct((TOP_K, t), jnp.int32),
                   jax.ShapeDtypeStruct((TOP_K, t), jnp.float32), jax.ShapeDtypeStruct((ne, 128), jnp.float32)],
        scratch_shapes=[pltpu.VMEM((ne, 1), jnp.float32)],
        compiler_params=_cp(("arbitrary",)), name="moe_router",
    )(hn, _bf16(router_w.T), router_bias.reshape(ne, 1).astype(jnp.float32), ut, lt)


def _dest_kernel(start_ref, eid_ref, pos_ref, o_ref):
    eid = eid_ref[...]
    dest = pos_ref[...]
    for e in range(start_ref.shape[0]):
        dest = dest + jnp.where(eid == e, start_ref[e], 0)
    o_ref[...] = dest


def moe_dest(pad_start, eid, posk):
    return pl.pallas_call(
        _dest_kernel,
        grid_spec=pltpu.PrefetchScalarGridSpec(
            num_scalar_prefetch=1, grid=(1,),
            in_specs=[pl.BlockSpec(eid.shape, lambda i, s: (0, 0)), pl.BlockSpec(eid.shape, lambda i, s: (0, 0))],
            out_specs=pl.BlockSpec(eid.shape, lambda i, s: (0, 0))),
        out_shape=jax.ShapeDtypeStruct(eid.shape, jnp.int32),
        compiler_params=_cp(("arbitrary",)), name="moe_dest",
    )(pad_start, eid, posk)


def _dispatch_kernel(last_ref, dest_ref, src_ref, wg_ref, wu_ref, out_ref, hs_ref, zbuf, sem, zsem, *, tt):
    bm = zbuf.shape[0]

    @pl.when(pl.program_id(0) == 0)
    def _():
        zbuf[...] = jnp.zeros_like(zbuf)

        def zero_copy(e):
            return pltpu.make_async_copy(zbuf, out_ref.at[pl.ds(pl.multiple_of(last_ref[e], bm), bm)], zsem)

        def zstart(e, carry):
            @pl.when(last_ref[e] >= 0)
            def _():
                zero_copy(e).start()
            return carry

        def zwait(e, carry):
            @pl.when(last_ref[e] >= 0)
            def _():
                zero_copy(e).wait()
            return carry

        lax.fori_loop(0, last_ref.shape[0], zstart, 0)
        lax.fori_loop(0, last_ref.shape[0], zwait, 0)

    for t in range(tt):
        for k in range(TOP_K):
            pltpu.make_async_copy(src_ref.at[pl.ds(t, 1)], out_ref.at[pl.ds(dest_ref[k, t], 1)], sem).start()
    half = src_ref.shape[1]
    xp = src_ref[...]
    xa = _bf16(_unpack_hi(xp))
    xb = _bf16(_unpack_lo(xp))
    g = _dot(xa, wg_ref[0:half, :]) + _dot(xb, wg_ref[half:, :])
    u = _dot(xa, wu_ref[0:half, :]) + _dot(xb, wu_ref[half:, :])
    hs_ref[...] = _bf16(g * _sigmoid(g) * u)
    for k in range(TOP_K):
        pltpu.make_async_copy(src_ref, out_ref.at[pl.ds(0, tt)], sem).wait()


def moe_dispatch_shared(zero_blocks, dest, hp, s_gate, s_up, n_rows, bm, tt=256):
    t, w = hp.shape
    d, f = s_gate.shape
    tt = min(tt, t)
    grid_spec = pltpu.PrefetchScalarGridSpec(
        num_scalar_prefetch=1, grid=(t // tt,),
        in_specs=[
            pl.BlockSpec((TOP_K, tt), lambda i, last: (0, i), memory_space=pltpu.SMEM),
            pl.BlockSpec((tt, w), lambda i, last: (i, 0)),
            pl.BlockSpec((d, f), lambda i, last: (0, 0)),
            pl.BlockSpec((d, f), lambda i, last: (0, 0)),
        ],
        out_specs=[pl.BlockSpec(memory_space=pl.ANY), pl.BlockSpec((tt, f), lambda i, last: (i, 0))],
        scratch_shapes=[pltpu.VMEM((bm, w), jnp.uint32), pltpu.SemaphoreType.DMA(()), pltpu.SemaphoreType.DMA(())],
    )
    return pl.pallas_call(
        functools.partial(_dispatch_kernel, tt=tt), grid_spec=grid_spec,
        out_shape=[jax.ShapeDtypeStruct((n_rows, w), jnp.uint32), jax.ShapeDtypeStruct((t, f), jnp.bfloat16)],
        compiler_params=pltpu.CompilerParams(dimension_semantics=("arbitrary",), vmem_limit_bytes=V7X_VMEM_LIMIT,
                                             has_side_effects=True),
        name="moe_dispatch_shared",
    )(zero_blocks, dest, hp, s_gate, s_up)


def _expert_kernel(be_ref, nu_ref, x_ref, wg_ref, wu_ref, wd_ref, o_ref, wg_s, wu_s, wd_s):
    b = pl.program_id(0)
    e = be_ref[b]
    prev = be_ref[jnp.maximum(b - 1, 0)]
    half = x_ref.shape[1]

    @pl.when((b == 0) | (e != prev))
    def _():
        wg_s[0] = _bf16(wg_ref[0:half, :])
        wg_s[1] = _bf16(wg_ref[half:, :])
        wu_s[0] = _bf16(wu_ref[0:half, :])
        wu_s[1] = _bf16(wu_ref[half:, :])
        wd_s[...] = _bf16(wd_ref[...])

    @pl.when(b < nu_ref[0])
    def _():
        xp = x_ref[...]
        xa = _bf16(_unpack_hi(xp))
        xb = _bf16(_unpack_lo(xp))
        g = _dot(xa, wg_s[0]) + _dot(xb, wg_s[1])
        u = _dot(xa, wu_s[0]) + _dot(xb, wu_s[1])
        h = _bf16(g * _sigmoid(g) * u)
        y = _dot(h, wd_s[...])
        o_ref[...] = _pack_bf16_pair(y[:, :half], y[:, half:])

    @pl.when(b >= nu_ref[0])
    def _():
        o_ref[...] = jnp.zeros_like(o_ref)


def moe_experts(block_expert, n_used, xs, w_gate, w_up, w_down, layer, bm):
    n_rows, half = xs.shape
    _, ne, d, f = w_gate.shape
    nblk = n_rows // bm
    grid_spec = pltpu.PrefetchScalarGridSpec(
        num_scalar_prefetch=2, grid=(nblk,),
        in_specs=[
            pl.BlockSpec((bm, half), lambda b, be, nu: (jnp.minimum(b, nu[0] - 1), 0)),
            pl.BlockSpec((None, None, d, f), lambda b, be, nu: (layer, be[b], 0, 0)),
            pl.BlockSpec((None, None, d, f), lambda b, be, nu: (layer, be[b], 0, 0)),
            pl.BlockSpec((None, None, f, d), lambda b, be, nu: (layer, be[b], 0, 0)),
        ],
        out_specs=pl.BlockSpec((bm, half), lambda b, be, nu: (b, 0)),
        scratch_shapes=[pltpu.VMEM((2, half, f), jnp.bfloat16), pltpu.VMEM((2, half, f), jnp.bfloat16),
                        pltpu.VMEM((f, d), jnp.bfloat16)],
    )
    return pl.pallas_call(
        _expert_kernel, grid_spec=grid_spec,
        out_shape=jax.ShapeDtypeStruct((n_rows, half), jnp.uint32),
        compiler_params=_cp(("arbitrary",)), name="moe_experts",
    )(block_expert, n_used, xs, w_gate, w_up, w_down)


def _combine_kernel(dest_ref, nxt_ref, ys_ref, gk_ref, sh_ref, x_ref, ga_ref, o_ref, buf0, buf1, sems, *, tt, nt):
    i = pl.program_id(0)
    bufs = (buf0, buf1)
    half = buf0.shape[2]

    def row_copy(d_ref, s, k, t):
        return pltpu.make_async_copy(ys_ref.at[pl.ds(d_ref[k, t], 1)], bufs[s].at[k, pl.ds(t, 1)], sems.at[s])

    def wait_tile(s):
        for k in range(TOP_K):
            pltpu.make_async_copy(ys_ref.at[pl.ds(0, tt)], bufs[s].at[k], sems.at[s]).wait()

    @pl.when(i == 0)
    def _():
        def body(t, carry):
            for k in range(TOP_K):
                row_copy(dest_ref, 0, k, t).start()
            return carry

        lax.fori_loop(0, tt, body, 0)

    def step(cur, nxt):
        wait_tile(cur)
        for t in range(tt):
            for k in range(TOP_K):
                row_copy(nxt_ref, nxt, k, t).start()
        acc_a = jnp.zeros((tt, half), jnp.float32)
        acc_b = jnp.zeros((tt, half), jnp.float32)
        for k in range(TOP_K):
            p = bufs[cur][k]
            gk = gk_ref[:, k:k + 1]
            acc_a = acc_a + gk * _unpack_hi(p)
            acc_b = acc_b + gk * _unpack_lo(p)
        sh = sh_ref[...].astype(jnp.float32)
        o_ref[:, 0:half] = x_ref[:, 0:half] + ga_ref[:, 0:half] * (acc_a + sh[:, 0:half])
        o_ref[:, half:] = x_ref[:, half:] + ga_ref[:, half:] * (acc_b + sh[:, half:])

    @pl.when(i % 2 == 0)
    def _():
        step(0, 1)

    @pl.when(i % 2 == 1)
    def _():
        step(1, 0)

    @pl.when(i == nt - 1)
    def _():
        wait_tile(nt % 2)


def moe_combine(dest, ys, gates_tk, shared, x2d, gate_ada, seq, tt=128):
    t, d = x2d.shape
    half = ys.shape[1]
    tt = min(tt, seq)
    nt = t // tt
    return pl.pallas_call(
        functools.partial(_combine_kernel, tt=tt, nt=nt), grid=(nt,),
        in_specs=[
            pl.BlockSpec((TOP_K, tt), lambda i: (0, i), memory_space=pltpu.SMEM),
            pl.BlockSpec((TOP_K, tt), lambda i: (0, jnp.minimum(i + 1, nt - 1)), memory_space=pltpu.SMEM),
            pl.BlockSpec(memory_space=pl.ANY),
            pl.BlockSpec((tt, TOP_K), lambda i: (i, 0)),
            pl.BlockSpec((tt, d), lambda i: (i, 0)),
            pl.BlockSpec((tt, d), lambda i: (i, 0)),
            pl.BlockSpec((None, 1, d), lambda i: ((i * tt) // seq, 0, 0)),
        ],
        out_specs=pl.BlockSpec((tt, d), lambda i: (i, 0)),
        out_shape=jax.ShapeDtypeStruct((t, d), jnp.float32),
        scratch_shapes=[pltpu.VMEM((TOP_K, tt, half), jnp.uint32), pltpu.VMEM((TOP_K, tt, half), jnp.uint32),
                        pltpu.SemaphoreType.DMA((2,))],
        compiler_params=_cp(("arbitrary",)), name="moe_combine",
    )(dest, dest, ys, gates_tk, shared, x2d, gate_ada)


MOE_ROWS = 256


def moe_layer(x2d, hn, hp, gate_ada, seq, router_w, router_bias, w_gate, w_up, w_down, layer, s_gate, s_up, s_down):
    t, d = x2d.shape
    ne = router_w.shape[1]
    bm = min(MOE_ROWS, t)
    eid, posk, gates_k, cnt = moe_router(hn, router_w, router_bias)
    counts = cnt[:, 0].astype(jnp.int32)
    padded = (counts + bm - 1) // bm * bm
    pad_end = jnp.cumsum(padded)
    pad_start = pad_end - padded
    n_rows = t * TOP_K + ne * bm
    nblk = n_rows // bm
    dest = moe_dest(pad_start.astype(jnp.int32), eid, posk)
    block_row = jnp.arange(nblk, dtype=jnp.int32) * bm
    block_expert = jnp.minimum(jnp.sum(pad_end[None, :] <= block_row[:, None], axis=1), ne - 1).astype(jnp.int32)
    n_used = (pad_end[-1:] // bm).astype(jnp.int32)
    tail = (n_used[0] + jnp.arange(ne, dtype=jnp.int32)) * bm
    zero_blocks = jnp.concatenate([jnp.where(padded > counts, pad_end - bm, -1),
                                   jnp.where(tail < n_rows, tail, -1)]).astype(jnp.int32)
    xs, hs = moe_dispatch_shared(zero_blocks, dest, hp, _bf16(s_gate), _bf16(s_up), n_rows, bm)
    ys = moe_experts(block_expert, n_used, xs, w_gate, w_up, w_down, layer, bm)
    shared = matmul(hs, _bf16(s_down), jnp.bfloat16)
    return moe_combine(dest, ys, gates_k.T, shared, x2d, gate_ada, seq)


def kernel(x, c, mix_norm_g, mix_ada_w, mix_ada_b, ev_w_in, ev_conv_w, ev_conv_b, ev_ln_g, ev_ln_b, ev_q_norm_g, ev_k_norm_g, ev_w_out, od_w_in, od_log_dt, od_lambda_re, od_lambda_im, od_b_re, od_b_im, od_c_re, od_c_im, od_d, od_w_out, ffn_norm_g, ffn_ada_w, ffn_ada_b, router_w, router_bias, exp_gate, exp_up, exp_down, sh_gate, sh_up, sh_down):
    bsz, seq, d = x.shape
    depth = mix_norm_g.shape[0]
    t = bsz * seq
    c_pad = jnp.zeros((8, d), jnp.float32).at[:bsz].set(c)
    mix_mod = adaln_all(c_pad, mix_ada_w, mix_ada_b)[:, :bsz]
    ffn_mod = adaln_all(c_pad, ffn_ada_w, ffn_ada_b)[:, :bsz]

    def split(mod):
        return tuple(mod[:, None, k * d:(k + 1) * d] for k in range(3))

    for i in range(depth):
        j = i // 2
        shift, scale, gate = split(mix_mod[i])
        hn = norm_modulate(x, mix_norm_g[i], scale, shift)
        x2d = x.reshape(t, d)
        if i % 2 == 0:
            cc = ev_conv_w.shape[-1]
            dh = ev_q_norm_g.shape[-1]
            sbw = (ev_w_in.shape[-1] - 2 * cc) // 3
            proj = matmul(hn, _bf16(ev_w_in[j]), jnp.bfloat16)
            a = conv_branch(proj, ev_conv_w[j], ev_conv_b[j], ev_ln_g[j], ev_ln_b[j], seq)
            qn, kn = qk_norm(proj, ev_q_norm_g[j], ev_k_norm_g[j], 2 * cc, sbw)
            o = stickbreak_attention(qn, kn, proj, 2 * cc + 2 * sbw, bsz, seq, dh)
            x2d = matmul2_resid(a, o, _bf16(ev_w_out[j]), x2d, gate, seq)
        else:
            u = matmul(hn, _bf16(od_w_in[j]), jnp.float32)
            wts = s5_weights(od_log_dt[j], od_lambda_re[j], od_lambda_im[j], od_b_re[j], od_b_im[j],
                             od_c_re[j], od_c_im[j], od_d[j])
            y = s5_scan(u.reshape(bsz, seq, -1), *wts)
            x2d = matmul_glu_resid(y.reshape(t, -1), _bf16(od_w_out[j]), x2d, gate, seq)
        x = x2d.reshape(bsz, seq, d)
        shift, scale, gate = split(ffn_mod[i])
        hn, hp = norm_modulate(x, ffn_norm_g[i], scale, shift, packed=True)
        x2d = moe_layer(x.reshape(t, d), hn, hp, gate, seq, router_w[i], router_bias[i], exp_gate, exp_up,
                        exp_down, i, sh_gate[i], sh_up[i], sh_down[i])
        x = x2d.reshape(bsz, seq, d)
    return x
```
